```python
import math
import jax, jax.numpy as jnp
from jax import lax
import numpy as np

D_MODEL = 1024
BATCH = 16
SEQ = 256
DEPTH = 2
DEC_BATCH = 8
DEC_SEQ = 1024
PAST_LEN = 256

GRID_W = 64
MIX_WIDTH = D_MODEL
ATTN_HEADS = 4
ATTN_DH = 64
ATTN_WIDTH = ATTN_HEADS * 2 * ATTN_DH
CONV_CH = MIX_WIDTH // 4
CHUNK = 128
CHUNK_GROUPS = 4
CHUNK_CH = MIX_WIDTH // 4
CHUNK_GDIM = CHUNK_CH // CHUNK_GROUPS
D_FF = 2816
N_MOD = 9
Q_BLOCK = 128
ROPE_BASE = 10000.0
ROPE_FREQS = ATTN_DH // 4
QKV_W = ATTN_HEADS * 2 * ATTN_DH
IN_SPLITS = (QKV_W, 2 * QKV_W, 3 * QKV_W,
             3 * QKV_W + CONV_CH, 3 * QKV_W + 2 * CONV_CH, 3 * QKV_W + 3 * CONV_CH,
             3 * QKV_W + 3 * CONV_CH + CHUNK_CH)
IN_WIDTH = 3 * QKV_W + 3 * CONV_CH + 2 * CHUNK_CH

kernel_name = "hybrid_diffusion_prefix_trunk_step"


def rms_norm(x, g, eps=1e-6):
    xf = x.astype(jnp.float32)
    y = xf * lax.rsqrt(jnp.mean(xf * xf, axis=-1, keepdims=True) + eps)
    return (y * g.astype(jnp.float32)).astype(x.dtype)


def swiglu(h, w_gu, w_d):
    g, u = jnp.split(h @ w_gu, 2, axis=-1)
    return (jax.nn.silu(g) * u) @ w_d


def axial_rope_tables(n_tokens):
    n_rows = n_tokens // GRID_W
    row = jnp.repeat(jnp.arange(n_rows, dtype=jnp.float32), GRID_W)
    col = jnp.tile(jnp.arange(GRID_W, dtype=jnp.float32), n_rows)
    inv = ROPE_BASE ** (-jnp.arange(ROPE_FREQS, dtype=jnp.float32) / ROPE_FREQS)
    ang = jnp.concatenate([row[:, None] * inv, col[:, None] * inv], axis=-1)
    return jnp.cos(ang), jnp.sin(ang)


def apply_rope(x, cos, sin):
    cos = cos[:, None, :].astype(x.dtype)
    sin = sin[:, None, :].astype(x.dtype)
    x1, x2 = jnp.split(x, 2, axis=-1)
    return jnp.concatenate([x1 * cos - x2 * sin, x2 * cos + x1 * sin], axis=-1)


def diff_attention(q, k, v, lam):
    B, H, Sq = q.shape[:3]
    nb = Sq // Q_BLOCK
    qb = q.reshape(B, H, nb, Q_BLOCK, 2, ATTN_DH).transpose(2, 0, 1, 3, 4, 5)
    scale = ATTN_DH ** -0.5

    def one_block(q_blk):
        s = jnp.einsum('bhqmd,bhkmd->bhmqk', q_blk, k).astype(jnp.float32) * scale
        p = jax.nn.softmax(s, axis=-1)
        a = p[:, :, 0] - lam * p[:, :, 1]
        return jnp.einsum('bhqk,bhkd->bhqd', a.astype(v.dtype), v)

    o = lax.map(one_block, qb)
    return o.transpose(1, 2, 0, 3, 4).reshape(B, H, Sq, 2 * ATTN_DH)


def token_mixers(h, p, layer_idx, rope, ctx_kv):
    B, S, _ = h.shape
    z = h @ p['w_in']
    q, k, v, gb, gc, hc, u, vc = jnp.split(z, IN_SPLITS, axis=-1)

    q = q.reshape(B, S, ATTN_HEADS, 2, ATTN_DH).transpose(0, 2, 1, 3, 4)
    k = k.reshape(B, S, ATTN_HEADS, 2, ATTN_DH).transpose(0, 2, 1, 3, 4)
    v = v.reshape(B, S, ATTN_HEADS, 2 * ATTN_DH).transpose(0, 2, 1, 3)
    if rope is None:
        keys, vals = k, v
        new_kv = (k.reshape(B, ATTN_HEADS, S, 2 * ATTN_DH), v)
    else:
        cos, sin = rope
        q = apply_rope(q, cos, sin)
        k = apply_rope(k, cos, sin)
        ck, cv = ctx_kv
        P = ck.shape[2]
        keys = jnp.concatenate([ck.reshape(B, ATTN_HEADS, P, 2, ATTN_DH).astype(k.dtype), k], axis=2)
        vals = jnp.concatenate([cv.astype(v.dtype), v], axis=2)
        new_kv = None
    lam_p = p['attn_lam'].astype(jnp.float32)
    lam_init = 0.8 - 0.6 * math.exp(-0.3 * layer_idx)
    lam = jnp.exp(jnp.sum(lam_p[0] * lam_p[1])) - jnp.exp(jnp.sum(lam_p[2] * lam_p[3])) + lam_init
    o = diff_attention(q, keys, vals, lam)
    o = rms_norm(o, p['attn_subln_g'][:, None, :]) * (1.0 - lam_init)
    y_attn = o.transpose(0, 2, 1, 3).reshape(B, S, ATTN_WIDTH)

    xc = gc * hc
    xp = jnp.pad(xc, ((0, 0), (1, 1), (0, 0)))
    w = p['conv_w']
    y_conv = gb * (w[0] * xp[:, :-2] + w[1] * xp[:, 1:-1] + w[2] * xp[:, 2:])

    nck = S // CHUNK
    vr = vc.reshape(B, nck, CHUNK, CHUNK_GROUPS, CHUNK_GDIM)
    sp = jnp.einsum('gpq,bnqgc->bnpgc', p['chunk_ws'], vr) + p['chunk_b'].T[None, None, :, :, None]
    y_chunk = u * sp.reshape(B, S, CHUNK_CH)

    y = jnp.concatenate([y_attn, y_conv, y_chunk], axis=-1) @ p['w_out']
    return y, new_kv


def trunk_layer(x, cond, p, layer_idx, rope, ctx_kv):
    m = jax.nn.silu(cond) @ p['w_mod'] + p['b_mod']
    sh1, sc1, g1, sh2, sc2, g2, sh3, sc3, g3 = jnp.split(m[:, None, :], N_MOD, axis=-1)
    h = rms_norm(x, p['norm_g'][0]) * (1.0 + sc1) + sh1
    x = x + 0.5 * g1 * swiglu(h, p['ffn1_w_gu'], p['ffn1_w_d'])
    h = rms_norm(x, p['norm_g'][1]) * (1.0 + sc2) + sh2
    y, new_kv = token_mixers(h, p, layer_idx, rope, ctx_kv)
    x = x + g2 * y
    h = rms_norm(x, p['norm_g'][2]) * (1.0 + sc3) + sh3
    x = x + 0.5 * g3 * swiglu(h, p['ffn2_w_gu'], p['ffn2_w_d'])
    return x, new_kv


def setup_inputs(seed: int = 0) -> dict:
    key = jax.random.key(seed)
    ks = jax.random.split(key, 24)
    f32 = jnp.float32

    def nrm(k, shape, scale):
        return jax.random.normal(k, shape, f32) * scale

    D, F = D_MODEL, D_FF
    return {
        "x_prompt": nrm(ks[0], (BATCH, SEQ, D), 1.0),
        "x_sample": nrm(ks[1], (DEC_BATCH, DEC_SEQ, D), 1.0),
        "cache_k": nrm(ks[2], (DEC_BATCH, DEPTH, ATTN_HEADS, PAST_LEN, 2 * ATTN_DH), 1.0),
        "cache_v": nrm(ks[3], (DEC_BATCH, DEPTH, ATTN_HEADS, PAST_LEN, 2 * ATTN_DH), 1.0),
        "c": nrm(ks[4], (DEC_BATCH, D), 1.0),
        "c_ctx": nrm(ks[5], (D,), 1.0),
        "w_mod": nrm(ks[6], (DEPTH, D, N_MOD * D), 0.5 * D ** -0.5),
        "b_mod": nrm(ks[7], (DEPTH, N_MOD * D), 0.02),
        "norm_g": 1.0 + nrm(ks[8], (DEPTH, 3, D), 0.02),
        "ffn1_w_gu": nrm(ks[9], (DEPTH, D, 2 * F), D ** -0.5),
        "ffn1_w_d": nrm(ks[10], (DEPTH, F, D), F ** -0.5),
        "ffn2_w_gu": nrm(ks[11], (DEPTH, D, 2 * F), D ** -0.5),
        "ffn2_w_d": nrm(ks[12], (DEPTH, F, D), F ** -0.5),
        "w_in": nrm(ks[13], (DEPTH, D, IN_WIDTH), D ** -0.5),
        "w_out": nrm(ks[14], (DEPTH, MIX_WIDTH, D), MIX_WIDTH ** -0.5),
        "attn_lam": nrm(ks[15], (DEPTH, 4, ATTN_DH), 0.1),
        "attn_subln_g": 1.0 + nrm(ks[16], (DEPTH, ATTN_HEADS, 2 * ATTN_DH), 0.02),
        "conv_w": nrm(ks[17], (DEPTH, 3, CONV_CH), 3 ** -0.5),
        "chunk_ws": nrm(ks[18], (DEPTH, CHUNK_GROUPS, CHUNK, CHUNK), CHUNK ** -0.5),
        "chunk_b": 1.0 + nrm(ks[19], (DEPTH, CHUNK_GROUPS, CHUNK), 0.1),
        "final_norm_g": 1.0 + nrm(ks[20], (D,), 0.02),
    }


def reference(x_prompt, x_sample, cache_k, cache_v, c, c_ctx, w_mod, b_mod, norm_g,
              ffn1_w_gu, ffn1_w_d, ffn2_w_gu, ffn2_w_d, w_in, w_out, attn_lam,
              attn_subln_g, conv_w, chunk_ws, chunk_b, final_norm_g):
    def layer_params(l):
        return dict(w_mod=w_mod[l], b_mod=b_mod[l], norm_g=norm_g[l],
                    ffn1_w_gu=ffn1_w_gu[l], ffn1_w_d=ffn1_w_d[l],
                    ffn2_w_gu=ffn2_w_gu[l], ffn2_w_d=ffn2_w_d[l],
                    w_in=w_in[l], w_out=w_out[l], attn_lam=attn_lam[l],
                    attn_subln_g=attn_subln_g[l], conv_w=conv_w[l],
                    chunk_ws=chunk_ws[l], chunk_b=chunk_b[l])

    xp = x_prompt
    ks_list, vs_list = [], []
    cond_ctx = c_ctx[None, :]
    for l in range(DEPTH):
        xp, (k_l, v_l) = trunk_layer(xp, cond_ctx, layer_params(l), l, None, None)
        ks_list.append(k_l)
        vs_list.append(v_l)
    y_prompt = rms_norm(xp, final_norm_g)
    new_k = jnp.stack(ks_list, axis=1)
    new_v = jnp.stack(vs_list, axis=1)

    xs = x_sample
    rope = axial_rope_tables(xs.shape[1])
    for l in range(DEPTH):
        xs, _ = trunk_layer(xs, c, layer_params(l), l, rope, (cache_k[:, l], cache_v[:, l]))
    y_sample = rms_norm(xs, final_norm_g)

    return (y_prompt, y_sample, new_k, new_v)
```

```python
import functools
import math

import jax
import jax.numpy as jnp
from jax import lax
from jax.experimental import pallas as pl
from jax.experimental.pallas import tpu as pltpu

D_MODEL = 1024
D_FF = 2816
DEPTH = 2
N_MOD = 9
GRID_W = 64
ATTN_HEADS = 4
ATTN_DH = 64
HEAD_W = 2 * ATTN_DH
QKV_W = ATTN_HEADS * HEAD_W
CONV_CH = 256
CHUNK = 128
CHUNK_GROUPS = 4
CHUNK_CH = 256
CHUNK_GDIM = CHUNK_CH // CHUNK_GROUPS
IN_WIDTH = 3 * QKV_W + 3 * CONV_CH + 2 * CHUNK_CH
ROPE_BASE = 10000.0
ROPE_FREQS = ATTN_DH // 4
EPS = 1e-6

COND_ROWS = 16
MOD_TILE_N = 1152
FFN_TILE_M = 512
FFN_CHUNKS = ((0, 512), (512, 512), (1024, 512), (1536, 512), (2048, 512), (2560, 256))
Q_BLOCK = 256
CTX_SEQ_PER_STEP = 2
V7X_VMEM_LIMIT_BYTES = 56 * 1024 * 1024

BF16 = jnp.bfloat16
F32 = jnp.float32


def _dot(a, b):
    return jnp.dot(a, b, preferred_element_type=F32)


def _dot_nt(a, b):
    return lax.dot_general(a, b, (((1,), (1,)), ((), ())), preferred_element_type=F32)


def _rms(x):
    return x * lax.rsqrt(jnp.mean(x * x, axis=-1, keepdims=True) + EPS)


def _mod_norm(x, g, shift, scale):
    return (_rms(x) * g) * (1.0 + scale) + shift


def _mod_kernel(c_ref, w_ref, b_ref, o_ref):
    a = jax.nn.silu(c_ref[...]).astype(BF16)
    o_ref[...] = _dot(a, w_ref[...].astype(BF16)) + b_ref[...]


def _mod_call(cond, w_mod, b_mod):
    n = N_MOD * D_MODEL
    return pl.pallas_call(
        _mod_kernel,
        grid=(DEPTH, n // MOD_TILE_N),
        in_specs=[
            pl.BlockSpec((COND_ROWS, D_MODEL), lambda l, j: (0, 0)),
            pl.BlockSpec((None, D_MODEL, MOD_TILE_N), lambda l, j: (l, 0, j)),
            pl.BlockSpec((None, 1, MOD_TILE_N), lambda l, j: (l, 0, j)),
        ],
        out_specs=pl.BlockSpec((None, COND_ROWS, MOD_TILE_N), lambda l, j: (l, 0, j)),
        out_shape=jax.ShapeDtypeStruct((DEPTH, COND_ROWS, n), F32),
        compiler_params=pltpu.CompilerParams(dimension_semantics=("arbitrary", "arbitrary")),
        name="mod_rows",
    )(cond, w_mod, b_mod.reshape(DEPTH, 1, n))


def _ffn_kernel(x_ref, mod_ref, g_ref, wgu_ref, wd_ref, *rest, final_norm):
    if final_norm:
        fg_ref, o_ref = rest
    else:
        (o_ref,) = rest
    x = x_ref[...]
    m = mod_ref[0]
    shift, scale, gate = m[:, 0:D_MODEL], m[:, D_MODEL:2 * D_MODEL], m[:, 2 * D_MODEL:3 * D_MODEL]
    h = _mod_norm(x, g_ref[...], shift, scale).astype(BF16)
    acc = None
    for c0, cw in FFN_CHUNKS:
        g = _dot(h, wgu_ref[:, c0:c0 + cw])
        u = _dot(h, wgu_ref[:, D_FF + c0:D_FF + c0 + cw])
        a = (jax.nn.silu(g) * u).astype(BF16)
        part = _dot(a, wd_ref[c0:c0 + cw, :])
        acc = part if acc is None else acc + part
    y = x + 0.5 * gate * acc
    if final_norm:
        y = _rms(y) * fg_ref[...]
    o_ref[...] = y


def _ffn_call(x, mod, norm_g, w_gu, w_d, layer, rows_per_batch, final_g=None):
    rows = x.shape[0]
    tm = FFN_TILE_M
    resident = dict(pipeline_mode=pl.Buffered(1))
    in_specs = [
        pl.BlockSpec((tm, D_MODEL), lambda i: (i, 0)),
        pl.BlockSpec((1, 1, 3 * D_MODEL), lambda i: ((i * tm) // rows_per_batch, 0, 0)),
        pl.BlockSpec((1, D_MODEL), lambda i: (0, 0)),
        pl.BlockSpec((None, D_MODEL, 2 * D_FF), lambda i: (layer, 0, 0), **resident),
        pl.BlockSpec((None, D_FF, D_MODEL), lambda i: (layer, 0, 0), **resident),
    ]
    args = [x, mod, norm_g, w_gu, w_d]
    if final_g is not None:
        in_specs.append(pl.BlockSpec((1, D_MODEL), lambda i: (0, 0)))
        args.append(final_g)
    return pl.pallas_call(
        functools.partial(_ffn_kernel, final_norm=final_g is not None),
        grid=(rows // tm,),
        in_specs=in_specs,
        out_specs=pl.BlockSpec((tm, D_MODEL), lambda i: (i, 0)),
        out_shape=jax.ShapeDtypeStruct((rows, D_MODEL), F32),
        compiler_params=pltpu.CompilerParams(
            dimension_semantics=("arbitrary",), vmem_limit_bytes=V7X_VMEM_LIMIT_BYTES),
        name="ffn",
    )(*args)


def _softmax_parts(scores):
    mx = functools.reduce(jnp.maximum, [jnp.max(s, axis=-1, keepdims=True) for s in scores])
    es = [jnp.exp(s - mx) for s in scores]
    tot = functools.reduce(jnp.add, [jnp.sum(e, axis=-1, keepdims=True) for e in es])
    return es, 1.0 / tot


def _mixer_kernel(*refs, seq, nseq, lam_init, rope):
    (x_ref, mod_ref, g_ref, win_ref, wout_ref, lam_ref, subg_ref,
     convw_ref, ws_ref, cb_ref) = refs[:10]
    refs = refs[10:]
    if rope:
        ck_ref, cv_ref, cos_ref, sin_ref, o_ref = refs[:5]
        refs = refs[5:]
    else:
        o_ref, nk_ref, nv_ref = refs[:3]
        refs = refs[3:]
    q1_s, q2_s, k_s, v_s, y_s = refs
    rows = seq * nseq

    x = x_ref[...]
    m = mod_ref[0]
    shift, scale, gate = m[:, 0:D_MODEL], m[:, D_MODEL:2 * D_MODEL], m[:, 2 * D_MODEL:3 * D_MODEL]
    h = _mod_norm(x, g_ref[...], shift, scale).astype(BF16)

    lp = lam_ref[...]
    lam = (jnp.exp(jnp.sum(lp[0:1] * lp[1:2], axis=-1, keepdims=True))
           - jnp.exp(jnp.sum(lp[2:3] * lp[3:4], axis=-1, keepdims=True)) + lam_init)

    lane = lax.broadcasted_iota(jnp.int32, (rows, HEAD_W), 1)
    first_map = lane < ATTN_DH
    if rope:
        first_half = (lane & (ATTN_DH - 1)) < (ATTN_DH // 2)
        cos_t, sin_t = cos_ref[...], sin_ref[...]

        def rotate(t):
            partner = jnp.where(first_half, pltpu.roll(t, HEAD_W - ATTN_DH // 2, 1),
                                pltpu.roll(t, ATTN_DH // 2, 1))
            return t * cos_t + partner * sin_t

    qf = _dot(h, win_ref[:, 0:QKV_W])
    kf = _dot(h, win_ref[:, QKV_W:2 * QKV_W])
    vf = _dot(h, win_ref[:, 2 * QKV_W:3 * QKV_W])
    v_s[...] = vf.astype(BF16)
    for hh in range(ATTN_HEADS):
        sl = slice(hh * HEAD_W, (hh + 1) * HEAD_W)
        qh, kh = qf[:, sl], kf[:, sl]
        if rope:
            qh, kh = rotate(qh), rotate(kh)
        else:
            for s in range(nseq):
                nk_ref[s, hh] = kh[s * seq:(s + 1) * seq]
                nv_ref[s, hh] = vf[s * seq:(s + 1) * seq, sl]
        qh = qh * (ATTN_DH ** -0.5)
        q1_s[:, sl] = jnp.where(first_map, qh, 0.0).astype(BF16)
        q2_s[:, sl] = jnp.where(first_map, 0.0, qh).astype(BF16)
        k_s[:, sl] = kh.astype(BF16)

    gch = _dot(h, win_ref[:, 3 * QKV_W:3 * QKV_W + 3 * CONV_CH])
    gb, gc, hc = gch[:, 0:CONV_CH], gch[:, CONV_CH:2 * CONV_CH], gch[:, 2 * CONV_CH:3 * CONV_CH]
    xc = gc * hc
    pos = lax.broadcasted_iota(jnp.int32, (rows, CONV_CH), 0) & (seq - 1)
    prev = jnp.where(pos == 0, 0.0, pltpu.roll(xc, 1, 0))
    nxt = jnp.where(pos == seq - 1, 0.0, pltpu.roll(xc, rows - 1, 0))
    cw = convw_ref[...]
    y_s[:, QKV_W:QKV_W + CONV_CH] = (gb * (cw[0:1] * prev + cw[1:2] * xc + cw[2:3] * nxt)).astype(BF16)

    uv = _dot(h, win_ref[:, 3 * QKV_W + 3 * CONV_CH:IN_WIDTH])
    u, vc = uv[:, 0:CHUNK_CH], uv[:, CHUNK_CH:2 * CHUNK_CH]
    grp = lax.broadcasted_iota(jnp.int32, (CHUNK, CHUNK_CH), 1) // CHUNK_GDIM
    for n in range(rows // CHUNK):
        rs = slice(n * CHUNK, (n + 1) * CHUNK)
        vcn = vc[rs]
        sp = cb_ref[...]
        for g in range(CHUNK_GROUPS):
            sp = sp + _dot(ws_ref[g], jnp.where(grp == g, vcn, 0.0).astype(BF16))
        y_s[rs, QKV_W + CONV_CH:D_MODEL] = (u[rs] * sp).astype(BF16)

    def attend(row0):
        for s in range(nseq):
            base = s * seq
            for hh in range(ATTN_HEADS):
                sl = slice(hh * HEAD_W, (hh + 1) * HEAD_W)
                kh = k_s[base:base + seq, sl]
                vh = v_s[base:base + seq, sl]
                keys, vals = [kh], [vh]
                if rope:
                    keys.append(ck_ref[hh].astype(BF16))
                    vals.append(cv_ref[hh].astype(BF16))
                qrows = pl.ds(base + row0, Q_BLOCK)
                e1, r1 = _softmax_parts([_dot_nt(q1_s[qrows, sl], kk) for kk in keys])
                e2, r2 = _softmax_parts([_dot_nt(q2_s[qrows, sl], kk) for kk in keys])
                c2 = lam * r2
                o = None
                for ea, eb, vv in zip(e1, e2, vals):
                    part = _dot((ea * r1 - eb * c2).astype(BF16), vv)
                    o = part if o is None else o + part
                o = (_rms(o) * subg_ref[hh:hh + 1, :]) * (1.0 - lam_init)
                y_s[qrows, sl] = o.astype(BF16)

    nqb = seq // Q_BLOCK
    if nqb == 1:
        attend(0)
    else:
        def body(qb, carry):
            attend(pl.multiple_of(qb * Q_BLOCK, Q_BLOCK))
            return carry
        lax.fori_loop(0, nqb, body, 0)

    o_ref[...] = x + gate * _dot(y_s[...], wout_ref[...])


def _mixer_call(x, mod, norm_g, w_in, w_out, attn_lam, subln_g, conv_w, chunk_ws, chunk_bias,
                layer, seq, nseq, cache=None):
    rows_total = x.shape[0]
    rows = seq * nseq
    nsteps = rows_total // rows
    rope = cache is not None
    lam_init = 0.8 - 0.6 * math.exp(-0.3 * layer)
    resident = dict(pipeline_mode=pl.Buffered(1))
    const2 = lambda i: (0, 0)
    in_specs = [
        pl.BlockSpec((rows, D_MODEL), lambda i: (i, 0)),
        pl.BlockSpec((1, 1, 3 * D_MODEL), lambda i: ((i * rows) // (rows_total // mod.shape[0]), 0, 0)),
        pl.BlockSpec((1, D_MODEL), const2),
        pl.BlockSpec((None, D_MODEL, IN_WIDTH), lambda i: (layer, 0, 0), **resident),
        pl.BlockSpec((None, D_MODEL, D_MODEL), lambda i: (layer, 0, 0), **resident),
        pl.BlockSpec((None, 4, ATTN_DH), lambda i: (layer, 0, 0)),
        pl.BlockSpec((None, ATTN_HEADS, HEAD_W), lambda i: (layer, 0, 0)),
        pl.BlockSpec((None, 3, CONV_CH), lambda i: (layer, 0, 0)),
        pl.BlockSpec((None, CHUNK_GROUPS, CHUNK, CHUNK), lambda i: (layer, 0, 0, 0)),
        pl.BlockSpec((None, CHUNK, CHUNK_CH), lambda i: (layer, 0, 0)),
    ]
    args = [x, mod, norm_g, w_in, w_out, attn_lam, subln_g, conv_w, chunk_ws, chunk_bias]
    x_out = jax.ShapeDtypeStruct((rows_total, D_MODEL), F32)
    x_spec = pl.BlockSpec((rows, D_MODEL), lambda i: (i, 0))
    if rope:
        cache_k, cache_v, cos_t, sin_t = cache
        past = cache_k.shape[3]
        kv_spec = pl.BlockSpec((None, None, ATTN_HEADS, past, HEAD_W), lambda i: (i, layer, 0, 0, 0))
        in_specs += [kv_spec, kv_spec,
                     pl.BlockSpec((seq, HEAD_W), const2), pl.BlockSpec((seq, HEAD_W), const2)]
        args += [cache_k, cache_v, cos_t, sin_t]
        out_shape, out_specs = x_out, x_spec
    else:
        nb = rows_total // seq
        kv_out = jax.ShapeDtypeStruct((nb, ATTN_HEADS, seq, HEAD_W), F32)
        kv_spec = pl.BlockSpec((nseq, ATTN_HEADS, seq, HEAD_W), lambda i: (i, 0, 0, 0))
        out_shape, out_specs = (x_out, kv_out, kv_out), (x_spec, kv_spec, kv_spec)
    return pl.pallas_call(
        functools.partial(_mixer_kernel, seq=seq, nseq=nseq, lam_init=lam_init, rope=rope),
        grid=(nsteps,),
        in_specs=in_specs,
        out_specs=out_specs,
        out_shape=out_shape,
        scratch_shapes=[pltpu.VMEM((rows, QKV_W), BF16)] * 4 + [pltpu.VMEM((rows, D_MODEL), BF16)],
        compiler_params=pltpu.CompilerParams(
            dimension_semantics=("arbitrary",), vmem_limit_bytes=V7X_VMEM_LIMIT_BYTES),
        name="mixer_latent" if rope else "mixer_context",
    )(*args)


def _rope_tables(n_tokens):
    n_rows = n_tokens // GRID_W
    row = jnp.repeat(jnp.arange(n_rows, dtype=F32), GRID_W)
    col = jnp.tile(jnp.arange(GRID_W, dtype=F32), n_rows)
    inv = ROPE_BASE ** (-jnp.arange(ROPE_FREQS, dtype=F32) / ROPE_FREQS)
    ang = jnp.concatenate([row[:, None] * inv, col[:, None] * inv], axis=-1)
    cos, sin = jnp.cos(ang), jnp.sin(ang)
    return jnp.tile(cos, (1, 4)), jnp.tile(jnp.concatenate([-sin, sin], axis=-1), (1, 2))


def kernel(x_prompt, x_sample, cache_k, cache_v, c, c_ctx, w_mod, b_mod, norm_g, ffn1_w_gu, ffn1_w_d,
           ffn2_w_gu, ffn2_w_d, w_in, w_out, attn_lam, attn_subln_g, conv_w, chunk_ws, chunk_b,
           final_norm_g):
    nb_ctx, seq_ctx, _ = x_prompt.shape
    nb_lat, seq_lat, _ = x_sample.shape

    cond = jnp.concatenate(
        [c_ctx[None, :], c, jnp.zeros((COND_ROWS - 1 - nb_lat, D_MODEL), F32)], axis=0)
    mods = _mod_call(cond, w_mod, b_mod)

    def mod_rows(layer, sub, stream):
        cols = slice(3 * sub * D_MODEL, 3 * (sub + 1) * D_MODEL)
        r = mods[layer, 0:1, cols] if stream == 0 else mods[layer, 1:1 + nb_lat, cols]
        return r[:, None, :]

    bf = lambda w: w.astype(BF16)
    ffn1_w_gu, ffn1_w_d, ffn2_w_gu, ffn2_w_d = bf(ffn1_w_gu), bf(ffn1_w_d), bf(ffn2_w_gu), bf(ffn2_w_d)
    w_in, w_out, chunk_ws = bf(w_in), bf(w_out), bf(chunk_ws)
    chunk_bias = jnp.repeat(jnp.swapaxes(chunk_b, 1, 2), CHUNK_GDIM, axis=2)
    rope = _rope_tables(seq_lat)
    final_g = final_norm_g[None, :]

    xp = x_prompt.reshape(nb_ctx * seq_ctx, D_MODEL)
    xs = x_sample.reshape(nb_lat * seq_lat, D_MODEL)
    new_k, new_v = [], []
    for l in range(DEPTH):
        last = l == DEPTH - 1
        ng = lambda k: norm_g[l, k][None, :]
        xp = _ffn_call(xp, mod_rows(l, 0, 0), ng(0), ffn1_w_gu, ffn1_w_d, l, nb_ctx * seq_ctx)
        xs = _ffn_call(xs, mod_rows(l, 0, 1), ng(0), ffn1_w_gu, ffn1_w_d, l, seq_lat)
        xp, nk, nv = _mixer_call(xp, mod_rows(l, 1, 0), ng(1), w_in, w_out, attn_lam, attn_subln_g,
                                 conv_w, chunk_ws, chunk_bias, l, seq_ctx, CTX_SEQ_PER_STEP)
        xs = _mixer_call(xs, mod_rows(l, 1, 1), ng(1), w_in, w_out, attn_lam, attn_subln_g,
                         conv_w, chunk_ws, chunk_bias, l, seq_lat, 1,
                         cache=(cache_k, cache_v) + rope)
        new_k.append(nk)
        new_v.append(nv)
        fg = final_g if last else None
        xp = _ffn_call(xp, mod_rows(l, 2, 0), ng(2), ffn2_w_gu, ffn2_w_d, l, nb_ctx * seq_ctx, fg)
        xs = _ffn_call(xs, mod_rows(l, 2, 1), ng(2), ffn2_w_gu, ffn2_w_d, l, seq_lat, fg)

    return (xp.reshape(nb_ctx, seq_ctx, D_MODEL), xs.reshape(nb_lat, seq_lat, D_MODEL),
            jnp.stack(new_k, axis=1), jnp.stack(new_v, axis=1))
```

```python
import functools
import math

import jax
import jax.numpy as jnp
from jax import lax
from jax.experimental import pallas as pl
from jax.experimental.pallas import tpu as pltpu

D_MODEL = 1024
D_FF = 2816
DEPTH = 2
N_MOD = 9
GRID_W = 64
ATTN_HEADS = 4
ATTN_DH = 64
HEAD_W = 2 * ATTN_DH
QKV_W = ATTN_HEADS * HEAD_W
CONV_CH = 256
CHUNK = 128
CHUNK_GROUPS = 4
CHUNK_CH = 256
CHUNK_GDIM = CHUNK_CH // CHUNK_GROUPS
IN_WIDTH = 3 * QKV_W + 3 * CONV_CH + 2 * CHUNK_CH
ROPE_BASE = 10000.0
ROPE_FREQS = ATTN_DH // 4
EPS = 1e-6
LOG2_E = math.log2(math.e)

COND_ROWS = 16
MOD_TILE_N = 1152
FFN_TILE_M = 512
FFN_CHUNKS = ((0, 512), (512, 512), (1024, 512), (1536, 512), (2048, 512), (2560, 256))
Q_BLOCK = 512
SCORE_ELEMS_ISSUED_AHEAD = 128 * 1024
CTX_SEQ_PER_STEP = 2
V7X_VMEM_LIMIT_BYTES = 56 * 1024 * 1024

BF16 = jnp.bfloat16
F32 = jnp.float32


def _dot(a, b):
    return jnp.dot(a, b, preferred_element_type=F32)


def _dot_nt(a, b):
    return lax.dot_general(a, b, (((1,), (1,)), ((), ())), preferred_element_type=F32)


def _rms(x):
    return x * lax.rsqrt(jnp.mean(x * x, axis=-1, keepdims=True) + EPS)


def _mod_norm(x, g, shift, scale):
    return (_rms(x) * g) * (1.0 + scale) + shift


def _mod_kernel(c_ref, w_ref, b_ref, o_ref):
    a = jax.nn.silu(c_ref[...]).astype(BF16)
    o_ref[...] = _dot(a, w_ref[...].astype(BF16)) + b_ref[...]


def _mod_call(cond, w_mod, b_mod):
    n = N_MOD * D_MODEL
    return pl.pallas_call(
        _mod_kernel,
        grid=(DEPTH, n // MOD_TILE_N),
        in_specs=[
            pl.BlockSpec((COND_ROWS, D_MODEL), lambda l, j: (0, 0)),
            pl.BlockSpec((None, D_MODEL, MOD_TILE_N), lambda l, j: (l, 0, j)),
            pl.BlockSpec((None, 1, MOD_TILE_N), lambda l, j: (l, 0, j)),
        ],
        out_specs=pl.BlockSpec((None, COND_ROWS, MOD_TILE_N), lambda l, j: (l, 0, j)),
        out_shape=jax.ShapeDtypeStruct((DEPTH, COND_ROWS, n), F32),
        compiler_params=pltpu.CompilerParams(dimension_semantics=("arbitrary", "arbitrary")),
        name="mod_rows",
    )(cond, w_mod, b_mod.reshape(DEPTH, 1, n))


def _ffn_kernel(x_ref, mod_ref, g_ref, wgu_ref, wd_ref, *rest, final_norm):
    if final_norm:
        fg_ref, o_ref = rest
    else:
        (o_ref,) = rest
    x = x_ref[...]
    m = mod_ref[0]
    shift, scale, gate = m[:, 0:D_MODEL], m[:, D_MODEL:2 * D_MODEL], m[:, 2 * D_MODEL:3 * D_MODEL]
    h = _mod_norm(x, g_ref[...], shift, scale).astype(BF16)
    acc = None
    for c0, cw in FFN_CHUNKS:
        g = _dot(h, wgu_ref[:, c0:c0 + cw])
        u = _dot(h, wgu_ref[:, D_FF + c0:D_FF + c0 + cw])
        a = (jax.nn.silu(g) * u).astype(BF16)
        part = _dot(a, wd_ref[c0:c0 + cw, :])
        acc = part if acc is None else acc + part
    y = x + 0.5 * gate * acc
    if final_norm:
        y = _rms(y) * fg_ref[...]
    o_ref[...] = y


def _ffn_call(x, mod, norm_g, w_gu, w_d, layer, rows_per_batch, final_g=None):
    rows = x.shape[0]
    tm = FFN_TILE_M
    resident = dict(pipeline_mode=pl.Buffered(1))
    in_specs = [
        pl.BlockSpec((tm, D_MODEL), lambda i: (i, 0)),
        pl.BlockSpec((1, 1, 3 * D_MODEL), lambda i: ((i * tm) // rows_per_batch, 0, 0)),
        pl.BlockSpec((1, D_MODEL), lambda i: (0, 0)),
        pl.BlockSpec((None, D_MODEL, 2 * D_FF), lambda i: (layer, 0, 0), **resident),
        pl.BlockSpec((None, D_FF, D_MODEL), lambda i: (layer, 0, 0), **resident),
    ]
    args = [x, mod, norm_g, w_gu, w_d]
    if final_g is not None:
        in_specs.append(pl.BlockSpec((1, D_MODEL), lambda i: (0, 0)))
        args.append(final_g)
    return pl.pallas_call(
        functools.partial(_ffn_kernel, final_norm=final_g is not None),
        grid=(rows // tm,),
        in_specs=in_specs,
        out_specs=pl.BlockSpec((tm, D_MODEL), lambda i: (i, 0)),
        out_shape=jax.ShapeDtypeStruct((rows, D_MODEL), F32),
        compiler_params=pltpu.CompilerParams(
            dimension_semantics=("arbitrary",), vmem_limit_bytes=V7X_VMEM_LIMIT_BYTES),
        name="ffn",
    )(*args)


def _exp2_rowsum(s):
    e = jnp.exp2(s - jnp.max(s, axis=-1, keepdims=True))
    return e, jnp.sum(e, axis=-1, keepdims=True)


def _mixer_kernel(*refs, seq, nseq, lam_init, rope, layer):
    (x_ref, mod_ref, g_ref, win_ref, wout_ref, lam_ref, subg_ref,
     convw_ref, ws_ref, cb_ref) = refs[:10]
    refs = refs[10:]
    if rope:
        ck_ref, cv_ref, cos_ref, sin_ref, o_ref = refs[:5]
        refs = refs[5:]
    else:
        if layer > 0:
            pk_ref, pv_ref = refs[:2]
            refs = refs[2:]
        o_ref, nk_ref, nv_ref = refs[:3]
        refs = refs[3:]
        if layer > 0:
            nk_ref[:, 0:layer] = pk_ref[...]
            nv_ref[:, 0:layer] = pv_ref[...]
        nk_ref, nv_ref = nk_ref.at[:, layer], nv_ref.at[:, layer]
    q1_s, q2_s, k_s, v_s, y_s = refs
    rows = seq * nseq

    x = x_ref[...]
    m = mod_ref[0]
    shift, scale, gate = m[:, 0:D_MODEL], m[:, D_MODEL:2 * D_MODEL], m[:, 2 * D_MODEL:3 * D_MODEL]
    h = _mod_norm(x, g_ref[...], shift, scale).astype(BF16)

    lp = lam_ref[...]
    lam = (jnp.exp(jnp.sum(lp[0:1] * lp[1:2], axis=-1, keepdims=True))
           - jnp.exp(jnp.sum(lp[2:3] * lp[3:4], axis=-1, keepdims=True)) + lam_init)

    lane = lax.broadcasted_iota(jnp.int32, (rows, HEAD_W), 1)
    first_map = lane < ATTN_DH
    if rope:
        first_half = (lane & (ATTN_DH - 1)) < (ATTN_DH // 2)
        cos_t, sin_t = cos_ref[...], sin_ref[...]

        def rotate(t):
            partner = jnp.where(first_half, pltpu.roll(t, HEAD_W - ATTN_DH // 2, 1),
                                pltpu.roll(t, ATTN_DH // 2, 1))
            return t * cos_t + partner * sin_t

    qf = _dot(h, win_ref[:, 0:QKV_W])
    kf = _dot(h, win_ref[:, QKV_W:2 * QKV_W])
    vf = _dot(h, win_ref[:, 2 * QKV_W:3 * QKV_W])
    v_s[0:rows, :] = vf.astype(BF16)
    for hh in range(ATTN_HEADS):
        sl = slice(hh * HEAD_W, (hh + 1) * HEAD_W)
        qh, kh = qf[:, sl], kf[:, sl]
        if rope:
            qh, kh = rotate(qh), rotate(kh)
            k_s[rows:, sl] = ck_ref[hh].astype(BF16)
            v_s[rows:, sl] = cv_ref[hh].astype(BF16)
        else:
            for s in range(nseq):
                nk_ref[s, hh] = kh[s * seq:(s + 1) * seq]
                nv_ref[s, hh] = vf[s * seq:(s + 1) * seq, sl]
        qh = qh * (ATTN_DH ** -0.5 * LOG2_E)
        q1_s[:, sl] = jnp.where(first_map, qh, 0.0).astype(BF16)
        q2_s[:, sl] = jnp.where(first_map, 0.0, qh).astype(BF16)
        k_s[0:rows, sl] = kh.astype(BF16)

    gch = _dot(h, win_ref[:, 3 * QKV_W:3 * QKV_W + 3 * CONV_CH])
    gb, gc, hc = gch[:, 0:CONV_CH], gch[:, CONV_CH:2 * CONV_CH], gch[:, 2 * CONV_CH:3 * CONV_CH]
    xc = gc * hc
    pos = lax.broadcasted_iota(jnp.int32, (rows, CONV_CH), 0) & (seq - 1)
    prev = jnp.where(pos == 0, 0.0, pltpu.roll(xc, 1, 0))
    nxt = jnp.where(pos == seq - 1, 0.0, pltpu.roll(xc, rows - 1, 0))
    cw = convw_ref[...]
    y_s[:, QKV_W:QKV_W + CONV_CH] = (gb * (cw[0:1] * prev + cw[1:2] * xc + cw[2:3] * nxt)).astype(BF16)

    uv = _dot(h, win_ref[:, 3 * QKV_W + 3 * CONV_CH:IN_WIDTH])
    u, vc = uv[:, 0:CHUNK_CH], uv[:, CHUNK_CH:2 * CHUNK_CH]
    grp = lax.broadcasted_iota(jnp.int32, (CHUNK, CHUNK_CH), 1) // CHUNK_GDIM
    for n in range(rows // CHUNK):
        rs = slice(n * CHUNK, (n + 1) * CHUNK)
        vcn = vc[rs]
        sp = cb_ref[...]
        for g in range(CHUNK_GROUPS):
            sp = sp + _dot(ws_ref[g], jnp.where(grp == g, vcn, 0.0).astype(BF16))
        y_s[rs, QKV_W + CONV_CH:D_MODEL] = (u[rs] * sp).astype(BF16)

    kv_len = k_s.shape[0] // nseq
    qblk = min(Q_BLOCK, seq)
    units = [(s, hh) for s in range(nseq) for hh in range(ATTN_HEADS)]

    def attend(row0):
        def scores(unit):
            s, hh = unit
            sl = slice(hh * HEAD_W, (hh + 1) * HEAD_W)
            kh = k_s[s * kv_len:(s + 1) * kv_len, sl]
            qrows = pl.ds(s * seq + row0, qblk)
            return _dot_nt(q1_s[qrows, sl], kh), _dot_nt(q2_s[qrows, sl], kh)

        ahead = 1 if qblk * kv_len > SCORE_ELEMS_ISSUED_AHEAD else len(units)
        queue = [scores(unit) for unit in units[:ahead]]
        for i, (s, hh) in enumerate(units):
            s1, s2 = queue.pop(0)
            if i + ahead < len(units):
                queue.append(scores(units[i + ahead]))
            sl = slice(hh * HEAD_W, (hh + 1) * HEAD_W)
            vh = v_s[s * kv_len:(s + 1) * kv_len, sl]
            qrows = pl.ds(s * seq + row0, qblk)
            e1, z1 = _exp2_rowsum(s1)
            e2, z2 = _exp2_rowsum(s2)
            a = (e1 - e2 * (lam * z1 / z2)).astype(BF16)
            o = _dot(a, vh) * (1.0 / z1)
            o = (_rms(o) * subg_ref[hh:hh + 1, :]) * (1.0 - lam_init)
            y_s[qrows, sl] = o.astype(BF16)

    nqb = seq // qblk
    if nqb == 1:
        attend(0)
    else:
        def body(qb, carry):
            attend(pl.multiple_of(qb * qblk, qblk))
            return carry
        lax.fori_loop(0, nqb, body, 0)

    o_ref[...] = x + gate * _dot(y_s[...], wout_ref[...])


def _mixer_call(x, mod, norm_g, w_in, w_out, attn_lam, subln_g, conv_w, chunk_ws, chunk_bias,
                layer, seq, nseq, cache=None, prev_kv=None):
    rows_total = x.shape[0]
    rows = seq * nseq
    nsteps = rows_total // rows
    rope = cache is not None
    lam_init = 0.8 - 0.6 * math.exp(-0.3 * layer)
    resident = dict(pipeline_mode=pl.Buffered(1))
    const2 = lambda i: (0, 0)
    in_specs = [
        pl.BlockSpec((rows, D_MODEL), lambda i: (i, 0)),
        pl.BlockSpec((1, 1, 3 * D_MODEL), lambda i: ((i * rows) // (rows_total // mod.shape[0]), 0, 0)),
        pl.BlockSpec((1, D_MODEL), const2),
        pl.BlockSpec((None, D_MODEL, IN_WIDTH), lambda i: (layer, 0, 0), **resident),
        pl.BlockSpec((None, D_MODEL, D_MODEL), lambda i: (layer, 0, 0), **resident),
        pl.BlockSpec((None, 4, ATTN_DH), lambda i: (layer, 0, 0)),
        pl.BlockSpec((None, ATTN_HEADS, HEAD_W), lambda i: (layer, 0, 0)),
        pl.BlockSpec((None, 3, CONV_CH), lambda i: (layer, 0, 0)),
        pl.BlockSpec((None, CHUNK_GROUPS, CHUNK, CHUNK), lambda i: (layer, 0, 0, 0)),
        pl.BlockSpec((None, CHUNK, CHUNK_CH), lambda i: (layer, 0, 0)),
    ]
    args = [x, mod, norm_g, w_in, w_out, attn_lam, subln_g, conv_w, chunk_ws, chunk_bias]
    x_out = jax.ShapeDtypeStruct((rows_total, D_MODEL), F32)
    x_spec = pl.BlockSpec((rows, D_MODEL), lambda i: (i, 0))
    if rope:
        cache_k, cache_v, cos_t, sin_t = cache
        past = cache_k.shape[3]
        kv_spec = pl.BlockSpec((None, None, ATTN_HEADS, past, HEAD_W), lambda i: (i, layer, 0, 0, 0))
        in_specs += [kv_spec, kv_spec,
                     pl.BlockSpec((seq, HEAD_W), const2), pl.BlockSpec((seq, HEAD_W), const2)]
        args += [cache_k, cache_v, cos_t, sin_t]
        out_shape, out_specs = x_out, x_spec
        kv_rows = rows + past
    else:
        kv_rows = rows
        nb = rows_total // seq
        slab = lambda n: pl.BlockSpec((nseq, n, ATTN_HEADS, seq, HEAD_W), lambda i: (i, 0, 0, 0, 0))
        if layer > 0:
            in_specs += [slab(layer), slab(layer)]
            args += list(prev_kv)
        kv_out = jax.ShapeDtypeStruct((nb, layer + 1, ATTN_HEADS, seq, HEAD_W), F32)
        out_shape, out_specs = (x_out, kv_out, kv_out), (x_spec, slab(layer + 1), slab(layer + 1))
    return pl.pallas_call(
        functools.partial(_mixer_kernel, seq=seq, nseq=nseq, lam_init=lam_init, rope=rope, layer=layer),
        grid=(nsteps,),
        in_specs=in_specs,
        out_specs=out_specs,
        out_shape=out_shape,
        scratch_shapes=[pltpu.VMEM((rows, QKV_W), BF16)] * 2 + [pltpu.VMEM((kv_rows, QKV_W), BF16)] * 2
        + [pltpu.VMEM((rows, D_MODEL), BF16)],
        compiler_params=pltpu.CompilerParams(
            dimension_semantics=("arbitrary",), vmem_limit_bytes=V7X_VMEM_LIMIT_BYTES),
        name="mixer_latent" if rope else "mixer_context",
    )(*args)


def _rope_tables(n_tokens):
    n_rows = n_tokens // GRID_W
    row = jnp.repeat(jnp.arange(n_rows, dtype=F32), GRID_W)
    col = jnp.tile(jnp.arange(GRID_W, dtype=F32), n_rows)
    inv = ROPE_BASE ** (-jnp.arange(ROPE_FREQS, dtype=F32) / ROPE_FREQS)
    ang = jnp.concatenate([row[:, None] * inv, col[:, None] * inv], axis=-1)
    cos, sin = jnp.cos(ang), jnp.sin(ang)
    return jnp.tile(cos, (1, 4)), jnp.tile(jnp.concatenate([-sin, sin], axis=-1), (1, 2))


def kernel(x_prompt, x_sample, cache_k, cache_v, c, c_ctx, w_mod, b_mod, norm_g, ffn1_w_gu, ffn1_w_d,
           ffn2_w_gu, ffn2_w_d, w_in, w_out, attn_lam, attn_subln_g, conv_w, chunk_ws, chunk_b,
           final_norm_g):
    nb_ctx, seq_ctx, _ = x_prompt.shape
    nb_lat, seq_lat, _ = x_sample.shape

    cond = jnp.concatenate(
        [c_ctx[None, :], c, jnp.zeros((COND_ROWS - 1 - nb_lat, D_MODEL), F32)], axis=0)
    mods = _mod_call(cond, w_mod, b_mod)

    def mod_rows(layer, sub, stream):
        cols = slice(3 * sub * D_MODEL, 3 * (sub + 1) * D_MODEL)
        r = mods[layer, 0:1, cols] if stream == 0 else mods[layer, 1:1 + nb_lat, cols]
        return r[:, None, :]

    bf = lambda w: w.astype(BF16)
    ffn1_w_gu, ffn1_w_d, ffn2_w_gu, ffn2_w_d = bf(ffn1_w_gu), bf(ffn1_w_d), bf(ffn2_w_gu), bf(ffn2_w_d)
    w_in, w_out, chunk_ws = bf(w_in), bf(w_out), bf(chunk_ws)
    chunk_bias = jnp.repeat(jnp.swapaxes(chunk_b, 1, 2), CHUNK_GDIM, axis=2)
    rope = _rope_tables(seq_lat)
    final_g = final_norm_g[None, :]

    xp = x_prompt.reshape(nb_ctx * seq_ctx, D_MODEL)
    xs = x_sample.reshape(nb_lat * seq_lat, D_MODEL)
    new_kv = None
    for l in range(DEPTH):
        last = l == DEPTH - 1
        ng = lambda k: norm_g[l, k][None, :]
        xp = _ffn_call(xp, mod_rows(l, 0, 0), ng(0), ffn1_w_gu, ffn1_w_d, l, nb_ctx * seq_ctx)
        xs = _ffn_call(xs, mod_rows(l, 0, 1), ng(0), ffn1_w_gu, ffn1_w_d, l, seq_lat)
        xp, *new_kv = _mixer_call(xp, mod_rows(l, 1, 0), ng(1), w_in, w_out, attn_lam, attn_subln_g,
                                  conv_w, chunk_ws, chunk_bias, l, seq_ctx, CTX_SEQ_PER_STEP,
                                  prev_kv=new_kv)
        xs = _mixer_call(xs, mod_rows(l, 1, 1), ng(1), w_in, w_out, attn_lam, attn_subln_g,
                         conv_w, chunk_ws, chunk_bias, l, seq_lat, 1,
                         cache=(cache_k, cache_v) + rope)
        fg = final_g if last else None
        xp = _ffn_call(xp, mod_rows(l, 2, 0), ng(2), ffn2_w_gu, ffn2_w_d, l, nb_ctx * seq_ctx, fg)
        xs = _ffn_call(xs, mod_rows(l, 2, 1), ng(2), ffn2_w_gu, ffn2_w_d, l, seq_lat, fg)

    return (xp.reshape(nb_ctx, seq_ctx, D_MODEL), xs.reshape(nb_lat, seq_lat, D_MODEL), *new_kv)
```

```python
import functools
import math

import jax
import jax.numpy as jnp
from jax import lax
from jax.experimental import pallas as pl
from jax.experimental.pallas import tpu as pltpu

D_MODEL = 1024
D_FF = 2816
DEPTH = 2
N_MOD = 9
GRID_W = 64
ATTN_HEADS = 4
ATTN_DH = 64
HEAD_W = 2 * ATTN_DH
QKV_W = ATTN_HEADS * HEAD_W
CONV_CH = 256
CHUNK = 128
CHUNK_GROUPS = 4
CHUNK_CH = 256
CHUNK_GDIM = CHUNK_CH // CHUNK_GROUPS
IN_WIDTH = 3 * QKV_W + 3 * CONV_CH + 2 * CHUNK_CH
ROPE_BASE = 10000.0
ROPE_FREQS = ATTN_DH // 4
EPS = 1e-6
LOG2_E = math.log2(math.e)

COND_ROWS = 16
MOD_TILE_N = 1152
FFN_TILE_M = 512
FFN_CHUNKS = ((0, 512), (512, 512), (1024, 512), (1536, 512), (2048, 512), (2560, 256))
Q_BLOCK = 512
SCORE_ELEMS_ISSUED_AHEAD = 128 * 1024
CTX_SEQ_PER_STEP = 2
V7X_VMEM_LIMIT_BYTES = 56 * 1024 * 1024

BF16 = jnp.bfloat16
F32 = jnp.float32
BF16_SUBLANES = 16


def _dot(a, b):
    return jnp.dot(a, b, preferred_element_type=F32)


def _dot_nt(a, b):
    return lax.dot_general(a, b, (((1,), (1,)), ((), ())), preferred_element_type=F32)


def _rms(x):
    return x * lax.rsqrt(jnp.mean(x * x, axis=-1, keepdims=True) + EPS)


def _mod_norm(x, g, shift, scale):
    return (_rms(x) * g) * (1.0 + scale) + shift


def _mod_kernel(c_ref, w_ref, b_ref, o_ref):
    a = jax.nn.silu(c_ref[...]).astype(BF16)
    o_ref[...] = _dot(a, w_ref[...].astype(BF16)) + b_ref[...]


def _mod_call(cond, w_mod, b_mod):
    n = N_MOD * D_MODEL
    return pl.pallas_call(
        _mod_kernel,
        grid=(DEPTH, n // MOD_TILE_N),
        in_specs=[
            pl.BlockSpec((COND_ROWS, D_MODEL), lambda l, j: (0, 0)),
            pl.BlockSpec((None, D_MODEL, MOD_TILE_N), lambda l, j: (l, 0, j)),
            pl.BlockSpec((None, 1, MOD_TILE_N), lambda l, j: (l, 0, j)),
        ],
        out_specs=pl.BlockSpec((None, COND_ROWS, MOD_TILE_N), lambda l, j: (l, 0, j)),
        out_shape=jax.ShapeDtypeStruct((DEPTH, COND_ROWS, n), F32),
        compiler_params=pltpu.CompilerParams(dimension_semantics=("arbitrary", "arbitrary")),
        name="mod_rows",
    )(cond, w_mod, b_mod.reshape(DEPTH, 1, n))


def _cast_bands(src_refs, dst_refs):
    for src, dst in zip(src_refs, dst_refs):
        dst[...] = src[...].astype(BF16)


def _side_cast_io(side, nsteps):
    in_specs, out_specs, out_shapes = [], [], []
    for w, layer in side:
        _, r, c = w.shape
        band = r // nsteps
        assert band * nsteps == r and band % BF16_SUBLANES == 0, (w.shape, nsteps)
        in_specs.append(pl.BlockSpec((None, band, c), lambda i, layer=layer: (layer, i, 0)))
        out_specs.append(pl.BlockSpec((band, c), lambda i: (i, 0)))
        out_shapes.append(jax.ShapeDtypeStruct((r, c), BF16))
    return in_specs, out_specs, out_shapes


def _ffn_kernel(x_ref, mod_ref, g_ref, wgu_ref, wd_ref, *rest, final_norm, n_side):
    if final_norm:
        fg_ref, rest = rest[0], rest[1:]
    side_in, o_ref, side_out = rest[:n_side], rest[n_side], rest[n_side + 1:]
    _cast_bands(side_in, side_out)
    x = x_ref[...]
    m = mod_ref[0]
    shift, scale, gate = m[:, 0:D_MODEL], m[:, D_MODEL:2 * D_MODEL], m[:, 2 * D_MODEL:3 * D_MODEL]
    h = _mod_norm(x, g_ref[...], shift, scale).astype(BF16)
    acc = None
    for c0, cw in FFN_CHUNKS:
        g = _dot(h, wgu_ref[:, c0:c0 + cw])
        u = _dot(h, wgu_ref[:, D_FF + c0:D_FF + c0 + cw])
        a = (jax.nn.silu(g) * u).astype(BF16)
        part = _dot(a, wd_ref[c0:c0 + cw, :])
        acc = part if acc is None else acc + part
    y = x + 0.5 * gate * acc
    if final_norm:
        y = _rms(y) * fg_ref[...]
    o_ref[...] = y


def _ffn_call(x, mod, norm_g, w_gu, w_d, rows_per_batch, final_g=None, side=()):
    rows = x.shape[0]
    tm = FFN_TILE_M
    nsteps = rows // tm
    resident = dict(pipeline_mode=pl.Buffered(1))
    in_specs = [
        pl.BlockSpec((tm, D_MODEL), lambda i: (i, 0)),
        pl.BlockSpec((1, 1, 3 * D_MODEL), lambda i: ((i * tm) // rows_per_batch, 0, 0)),
        pl.BlockSpec((1, D_MODEL), lambda i: (0, 0)),
        pl.BlockSpec((D_MODEL, 2 * D_FF), lambda i: (0, 0), **resident),
        pl.BlockSpec((D_FF, D_MODEL), lambda i: (0, 0), **resident),
    ]
    args = [x, mod, norm_g, w_gu, w_d]
    if final_g is not None:
        in_specs.append(pl.BlockSpec((1, D_MODEL), lambda i: (0, 0)))
        args.append(final_g)
    side_in, side_out, side_shapes = _side_cast_io(side, nsteps)
    return pl.pallas_call(
        functools.partial(_ffn_kernel, final_norm=final_g is not None, n_side=len(side)),
        grid=(nsteps,),
        in_specs=in_specs + side_in,
        out_specs=[pl.BlockSpec((tm, D_MODEL), lambda i: (i, 0))] + side_out,
        out_shape=[jax.ShapeDtypeStruct((rows, D_MODEL), F32)] + side_shapes,
        compiler_params=pltpu.CompilerParams(
            dimension_semantics=("arbitrary",), vmem_limit_bytes=V7X_VMEM_LIMIT_BYTES),
        name="ffn",
    )(*args, *[w for w, _ in side])


def _exp2_rowsum(s):
    e = jnp.exp2(s - jnp.max(s, axis=-1, keepdims=True))
    return e, jnp.sum(e, axis=-1, keepdims=True)


def _mixer_kernel(*refs, seq, nseq, lam_init, rope, layer, n_side):
    (x_ref, mod_ref, g_ref, win_ref, wout_ref, lam_ref, subg_ref,
     convw_ref, ws_ref, cb_ref) = refs[:10]
    refs = refs[10:]
    if rope:
        ck_ref, cv_ref, cos_ref, sin_ref = refs[:4]
        refs = refs[4:]
    elif layer > 0:
        pk_ref, pv_ref = refs[:2]
        refs = refs[2:]
    side_in, o_ref, refs = refs[:n_side], refs[n_side], refs[n_side + 1:]
    if not rope:
        nk_ref, nv_ref = refs[:2]
        refs = refs[2:]
        if layer > 0:
            nk_ref[:, 0:layer] = pk_ref[...]
            nv_ref[:, 0:layer] = pv_ref[...]
        nk_ref, nv_ref = nk_ref.at[:, layer], nv_ref.at[:, layer]
    side_out, refs = refs[:n_side], refs[n_side:]
    _cast_bands(side_in, side_out)
    q1_s, q2_s, k_s, v_s, y_s = refs
    rows = seq * nseq

    x = x_ref[...]
    m = mod_ref[0]
    shift, scale, gate = m[:, 0:D_MODEL], m[:, D_MODEL:2 * D_MODEL], m[:, 2 * D_MODEL:3 * D_MODEL]
    h = _mod_norm(x, g_ref[...], shift, scale).astype(BF16)

    lp = lam_ref[...]
    lam = (jnp.exp(jnp.sum(lp[0:1] * lp[1:2], axis=-1, keepdims=True))
           - jnp.exp(jnp.sum(lp[2:3] * lp[3:4], axis=-1, keepdims=True)) + lam_init)

    lane = lax.broadcasted_iota(jnp.int32, (rows, HEAD_W), 1)
    first_map = lane < ATTN_DH
    if rope:
        first_half = (lane & (ATTN_DH - 1)) < (ATTN_DH // 2)
        cos_t, sin_t = cos_ref[...], sin_ref[...]

        def rotate(t):
            partner = jnp.where(first_half, pltpu.roll(t, HEAD_W - ATTN_DH // 2, 1),
                                pltpu.roll(t, ATTN_DH // 2, 1))
            return t * cos_t + partner * sin_t

    qf = _dot(h, win_ref[:, 0:QKV_W])
    kf = _dot(h, win_ref[:, QKV_W:2 * QKV_W])
    vf = _dot(h, win_ref[:, 2 * QKV_W:3 * QKV_W])
    v_s[0:rows, :] = vf.astype(BF16)
    for hh in range(ATTN_HEADS):
        sl = slice(hh * HEAD_W, (hh + 1) * HEAD_W)
        qh, kh = qf[:, sl], kf[:, sl]
        if rope:
            qh, kh = rotate(qh), rotate(kh)
            k_s[rows:, sl] = ck_ref[hh].astype(BF16)
            v_s[rows:, sl] = cv_ref[hh].astype(BF16)
        else:
            for s in range(nseq):
                nk_ref[s, hh] = kh[s * seq:(s + 1) * seq]
                nv_ref[s, hh] = vf[s * seq:(s + 1) * seq, sl]
        qh = qh * (ATTN_DH ** -0.5 * LOG2_E)
        q1_s[:, sl] = jnp.where(first_map, qh, 0.0).astype(BF16)
        q2_s[:, sl] = jnp.where(first_map, 0.0, qh).astype(BF16)
        k_s[0:rows, sl] = kh.astype(BF16)

    gch = _dot(h, win_ref[:, 3 * QKV_W:3 * QKV_W + 3 * CONV_CH])
    gb, gc, hc = gch[:, 0:CONV_CH], gch[:, CONV_CH:2 * CONV_CH], gch[:, 2 * CONV_CH:3 * CONV_CH]
    xc = gc * hc
    pos = lax.broadcasted_iota(jnp.int32, (rows, CONV_CH), 0) & (seq - 1)
    prev = jnp.where(pos == 0, 0.0, pltpu.roll(xc, 1, 0))
    nxt = jnp.where(pos == seq - 1, 0.0, pltpu.roll(xc, rows - 1, 0))
    cw = convw_ref[...]
    y_s[:, QKV_W:QKV_W + CONV_CH] = (gb * (cw[0:1] * prev + cw[1:2] * xc + cw[2:3] * nxt)).astype(BF16)

    uv = _dot(h, win_ref[:, 3 * QKV_W + 3 * CONV_CH:IN_WIDTH])
    u, vc = uv[:, 0:CHUNK_CH], uv[:, CHUNK_CH:2 * CHUNK_CH]
    grp = lax.broadcasted_iota(jnp.int32, (CHUNK, CHUNK_CH), 1) // CHUNK_GDIM
    for n in range(rows // CHUNK):
        rs = slice(n * CHUNK, (n + 1) * CHUNK)
        vcn = vc[rs]
        sp = cb_ref[...]
        for g in range(CHUNK_GROUPS):
            sp = sp + _dot(ws_ref[g], jnp.where(grp == g, vcn, 0.0).astype(BF16))
        y_s[rs, QKV_W + CONV_CH:D_MODEL] = (u[rs] * sp).astype(BF16)

    kv_len = k_s.shape[0] // nseq
    qblk = min(Q_BLOCK, seq)
    units = [(s, hh) for s in range(nseq) for hh in range(ATTN_HEADS)]

    def attend(row0):
        def scores(unit):
            s, hh = unit
            sl = slice(hh * HEAD_W, (hh + 1) * HEAD_W)
            kh = k_s[s * kv_len:(s + 1) * kv_len, sl]
            qrows = pl.ds(s * seq + row0, qblk)
            return _dot_nt(q1_s[qrows, sl], kh), _dot_nt(q2_s[qrows, sl], kh)

        ahead = 1 if qblk * kv_len > SCORE_ELEMS_ISSUED_AHEAD else len(units)
        queue = [scores(unit) for unit in units[:ahead]]
        for i, (s, hh) in enumerate(units):
            s1, s2 = queue.pop(0)
            if i + ahead < len(units):
                queue.append(scores(units[i + ahead]))
            sl = slice(hh * HEAD_W, (hh + 1) * HEAD_W)
            vh = v_s[s * kv_len:(s + 1) * kv_len, sl]
            qrows = pl.ds(s * seq + row0, qblk)
            e1, z1 = _exp2_rowsum(s1)
            e2, z2 = _exp2_rowsum(s2)
            a = (e1 - e2 * (lam * z1 / z2)).astype(BF16)
            o = _dot(a, vh) * (1.0 / z1)
            o = (_rms(o) * subg_ref[hh:hh + 1, :]) * (1.0 - lam_init)
            y_s[qrows, sl] = o.astype(BF16)

    nqb = seq // qblk
    if nqb == 1:
        attend(0)
    else:
        def body(qb, carry):
            attend(pl.multiple_of(qb * qblk, qblk))
            return carry
        lax.fori_loop(0, nqb, body, 0)

    o_ref[...] = x + gate * _dot(y_s[...], wout_ref[...])


def _mixer_call(x, mod, norm_g, w_in, w_out, attn_lam, subln_g, conv_w, chunk_ws, chunk_bias,
                layer, seq, nseq, cache=None, prev_kv=None, side=()):
    rows_total = x.shape[0]
    rows = seq * nseq
    nsteps = rows_total // rows
    rope = cache is not None
    lam_init = 0.8 - 0.6 * math.exp(-0.3 * layer)
    resident = dict(pipeline_mode=pl.Buffered(1))
    const2 = lambda i: (0, 0)
    in_specs = [
        pl.BlockSpec((rows, D_MODEL), lambda i: (i, 0)),
        pl.BlockSpec((1, 1, 3 * D_MODEL), lambda i: ((i * rows) // (rows_total // mod.shape[0]), 0, 0)),
        pl.BlockSpec((1, D_MODEL), const2),
        pl.BlockSpec((D_MODEL, IN_WIDTH), const2, **resident),
        pl.BlockSpec((D_MODEL, D_MODEL), const2, **resident),
        pl.BlockSpec((None, 4, ATTN_DH), lambda i: (layer, 0, 0)),
        pl.BlockSpec((None, ATTN_HEADS, HEAD_W), lambda i: (layer, 0, 0)),
        pl.BlockSpec((None, 3, CONV_CH), lambda i: (layer, 0, 0)),
        pl.BlockSpec((None, CHUNK_GROUPS, CHUNK, CHUNK), lambda i: (layer, 0, 0, 0)),
        pl.BlockSpec((None, CHUNK, CHUNK_CH), lambda i: (layer, 0, 0)),
    ]
    args = [x, mod, norm_g, w_in, w_out, attn_lam, subln_g, conv_w, chunk_ws, chunk_bias]
    x_out = jax.ShapeDtypeStruct((rows_total, D_MODEL), F32)
    x_spec = pl.BlockSpec((rows, D_MODEL), lambda i: (i, 0))
    if rope:
        cache_k, cache_v, cos_t, sin_t = cache
        past = cache_k.shape[3]
        kv_spec = pl.BlockSpec((None, None, ATTN_HEADS, past, HEAD_W), lambda i: (i, layer, 0, 0, 0))
        in_specs += [kv_spec, kv_spec,
                     pl.BlockSpec((seq, HEAD_W), const2), pl.BlockSpec((seq, HEAD_W), const2)]
        args += [cache_k, cache_v, cos_t, sin_t]
        out_shape, out_specs = [x_out], [x_spec]
        kv_rows = rows + past
    else:
        kv_rows = rows
        nb = rows_total // seq
        slab = lambda n: pl.BlockSpec((nseq, n, ATTN_HEADS, seq, HEAD_W), lambda i: (i, 0, 0, 0, 0))
        if layer > 0:
            in_specs += [slab(layer), slab(layer)]
            args += list(prev_kv)
        kv_out = jax.ShapeDtypeStruct((nb, layer + 1, ATTN_HEADS, seq, HEAD_W), F32)
        out_shape, out_specs = [x_out, kv_out, kv_out], [x_spec, slab(layer + 1), slab(layer + 1)]
    side_in, side_out, side_shapes = _side_cast_io(side, nsteps)
    args += [w for w, _ in side]
    return pl.pallas_call(
        functools.partial(_mixer_kernel, seq=seq, nseq=nseq, lam_init=lam_init, rope=rope, layer=layer,
                          n_side=len(side)),
        grid=(nsteps,),
        in_specs=in_specs + side_in,
        out_specs=out_specs + side_out,
        out_shape=out_shape + side_shapes,
        scratch_shapes=[pltpu.VMEM((rows, QKV_W), BF16)] * 2 + [pltpu.VMEM((kv_rows, QKV_W), BF16)] * 2
        + [pltpu.VMEM((rows, D_MODEL), BF16)],
        compiler_params=pltpu.CompilerParams(
            dimension_semantics=("arbitrary",), vmem_limit_bytes=V7X_VMEM_LIMIT_BYTES),
        name="mixer_latent" if rope else "mixer_context",
    )(*args)


def _rope_tables(n_tokens):
    n_rows = n_tokens // GRID_W
    row = jnp.repeat(jnp.arange(n_rows, dtype=F32), GRID_W)
    col = jnp.tile(jnp.arange(GRID_W, dtype=F32), n_rows)
    inv = ROPE_BASE ** (-jnp.arange(ROPE_FREQS, dtype=F32) / ROPE_FREQS)
    ang = jnp.concatenate([row[:, None] * inv, col[:, None] * inv], axis=-1)
    cos, sin = jnp.cos(ang), jnp.sin(ang)
    return jnp.tile(cos, (1, 4)), jnp.tile(jnp.concatenate([-sin, sin], axis=-1), (1, 2))


def kernel(x_prompt, x_sample, cache_k, cache_v, c, c_ctx, w_mod, b_mod, norm_g, ffn1_w_gu, ffn1_w_d,
           ffn2_w_gu, ffn2_w_d, w_in, w_out, attn_lam, attn_subln_g, conv_w, chunk_ws, chunk_b,
           final_norm_g):
    nb_ctx, seq_ctx, _ = x_prompt.shape
    nb_lat, seq_lat, _ = x_sample.shape

    cond = jnp.concatenate(
        [c_ctx[None, :], c, jnp.zeros((COND_ROWS - 1 - nb_lat, D_MODEL), F32)], axis=0)
    mods = _mod_call(cond, w_mod, b_mod)

    def mod_rows(layer, sub, stream):
        cols = slice(3 * sub * D_MODEL, 3 * (sub + 1) * D_MODEL)
        r = mods[layer, 0:1, cols] if stream == 0 else mods[layer, 1:1 + nb_lat, cols]
        return r[:, None, :]

    chunk_ws = chunk_ws.astype(BF16)
    chunk_bias = jnp.repeat(jnp.swapaxes(chunk_b, 1, 2), CHUNK_GDIM, axis=2)
    rope = _rope_tables(seq_lat)
    final_g = final_norm_g[None, :]

    xp = x_prompt.reshape(nb_ctx * seq_ctx, D_MODEL)
    xs = x_sample.reshape(nb_lat * seq_lat, D_MODEL)
    new_kv = None
    w1_gu, w1_d = ffn1_w_gu[0].astype(BF16), ffn1_w_d[0].astype(BF16)
    for l in range(DEPTH):
        last = l == DEPTH - 1
        ng = lambda k: norm_g[l, k][None, :]
        xp, = _ffn_call(xp, mod_rows(l, 0, 0), ng(0), w1_gu, w1_d, nb_ctx * seq_ctx)
        xs, w_in_l, w_out_l = _ffn_call(xs, mod_rows(l, 0, 1), ng(0), w1_gu, w1_d, seq_lat,
                                        side=[(w_in, l), (w_out, l)])
        xp, *new_kv, w2_gu, w2_d = _mixer_call(
            xp, mod_rows(l, 1, 0), ng(1), w_in_l, w_out_l, attn_lam, attn_subln_g, conv_w, chunk_ws,
            chunk_bias, l, seq_ctx, CTX_SEQ_PER_STEP, prev_kv=new_kv,
            side=[(ffn2_w_gu, l), (ffn2_w_d, l)])
        xs, = _mixer_call(xs, mod_rows(l, 1, 1), ng(1), w_in_l, w_out_l, attn_lam, attn_subln_g,
                          conv_w, chunk_ws, chunk_bias, l, seq_lat, 1, cache=(cache_k, cache_v) + rope)
        fg = final_g if last else None
        xp, = _ffn_call(xp, mod_rows(l, 2, 0), ng(2), w2_gu, w2_d, nb_ctx * seq_ctx, fg)
        xs, *next_w1 = _ffn_call(xs, mod_rows(l, 2, 1), ng(2), w2_gu, w2_d, seq_lat, fg,
                                 side=[] if last else [(ffn1_w_gu, l + 1), (ffn1_w_d, l + 1)])
        if not last:
            w1_gu, w1_d = next_w1

    return (xp.reshape(nb_ctx, seq_ctx, D_MODEL), xs.reshape(nb_lat, seq_lat, D_MODEL), *new_kv)
```

```python
import functools
import math

import jax
import jax.numpy as jnp
from jax import lax
from jax.experimental import pallas as pl
from jax.experimental.pallas import tpu as pltpu

D_MODEL = 1024
D_FF = 2816
DEPTH = 2
N_MOD = 9
GRID_W = 64
ATTN_HEADS = 4
ATTN_DH = 64
HEAD_W = 2 * ATTN_DH
QKV_W = ATTN_HEADS * HEAD_W
CONV_CH = 256
CHUNK = 128
CHUNK_GROUPS = 4
CHUNK_CH = 256
CHUNK_GDIM = CHUNK_CH // CHUNK_GROUPS
IN_WIDTH = 3 * QKV_W + 3 * CONV_CH + 2 * CHUNK_CH
ROPE_BASE = 10000.0
ROPE_FREQS = ATTN_DH // 4
EPS = 1e-6
LOG2_E = math.log2(math.e)

COND_ROWS = 16
MOD_TILE_N = 1152
FFN_TILE_M = 512
FFN_CHUNKS = ((0, 512), (512, 512), (1024, 512), (1536, 512), (2048, 512), (2560, 256))
Q_BLOCK = 512
SCORE_ELEMS_ISSUED_AHEAD = 128 * 1024
SCORE_UNITS_AHEAD = 1
CTX_SEQ_PER_STEP = 2
V7X_VMEM_LIMIT_BYTES = 56 * 1024 * 1024

BF16 = jnp.bfloat16
F32 = jnp.float32
BF16_SUBLANES = 16


def _dot(a, b):
    return jnp.dot(a, b, preferred_element_type=F32)


def _dot_nt(a, b):
    return lax.dot_general(a, b, (((1,), (1,)), ((), ())), preferred_element_type=F32)


def _rms(x):
    return x * lax.rsqrt(jnp.mean(x * x, axis=-1, keepdims=True) + EPS)


def _mod_norm(x, g, shift, scale):
    return (_rms(x) * g) * (1.0 + scale) + shift


def _mod_kernel(c_ref, w_ref, b_ref, o_ref):
    a = jax.nn.silu(c_ref[...]).astype(BF16)
    o_ref[...] = _dot(a, w_ref[...].astype(BF16)) + b_ref[...]


def _mod_call(cond, w_mod, b_mod):
    n = N_MOD * D_MODEL
    return pl.pallas_call(
        _mod_kernel,
        grid=(DEPTH, n // MOD_TILE_N),
        in_specs=[
            pl.BlockSpec((COND_ROWS, D_MODEL), lambda l, j: (0, 0)),
            pl.BlockSpec((None, D_MODEL, MOD_TILE_N), lambda l, j: (l, 0, j)),
            pl.BlockSpec((None, 1, MOD_TILE_N), lambda l, j: (l, 0, j)),
        ],
        out_specs=pl.BlockSpec((None, COND_ROWS, MOD_TILE_N), lambda l, j: (l, 0, j)),
        out_shape=jax.ShapeDtypeStruct((DEPTH, COND_ROWS, n), F32),
        compiler_params=pltpu.CompilerParams(dimension_semantics=("arbitrary", "arbitrary")),
        name="mod_rows",
    )(cond, w_mod, b_mod.reshape(DEPTH, 1, n))


def _cast_bands(src_refs, dst_refs):
    for src, dst in zip(src_refs, dst_refs):
        dst[...] = src[...].astype(BF16)


def _side_cast_io(side, nsteps):
    in_specs, out_specs, out_shapes = [], [], []
    for w, layer in side:
        _, r, c = w.shape
        band = r // nsteps
        assert band * nsteps == r and band % BF16_SUBLANES == 0, (w.shape, nsteps)
        in_specs.append(pl.BlockSpec((None, band, c), lambda i, layer=layer: (layer, i, 0)))
        out_specs.append(pl.BlockSpec((band, c), lambda i: (i, 0)))
        out_shapes.append(jax.ShapeDtypeStruct((r, c), BF16))
    return in_specs, out_specs, out_shapes


def _ffn_kernel(x_ref, mod_ref, g_ref, wgu_ref, wd_ref, *rest, final_norm, n_side):
    if final_norm:
        fg_ref, rest = rest[0], rest[1:]
    side_in, o_ref, side_out = rest[:n_side], rest[n_side], rest[n_side + 1:]
    _cast_bands(side_in, side_out)
    x = x_ref[...]
    m = mod_ref[0]
    shift, scale, gate = m[:, 0:D_MODEL], m[:, D_MODEL:2 * D_MODEL], m[:, 2 * D_MODEL:3 * D_MODEL]
    h = _mod_norm(x, g_ref[...], shift, scale).astype(BF16)
    acc = None
    for c0, cw in FFN_CHUNKS:
        g = _dot(h, wgu_ref[:, c0:c0 + cw].astype(BF16))
        u = _dot(h, wgu_ref[:, D_FF + c0:D_FF + c0 + cw].astype(BF16))
        a = (jax.nn.silu(g) * u).astype(BF16)
        part = _dot(a, wd_ref[c0:c0 + cw, :].astype(BF16))
        acc = part if acc is None else acc + part
    y = x + 0.5 * gate * acc
    if final_norm:
        y = _rms(y) * fg_ref[...]
    o_ref[...] = y


def _ffn_call(x, mod, norm_g, w_gu, w_d, rows_per_batch, final_g=None, side=(), tm=FFN_TILE_M):
    rows = x.shape[0]
    nsteps = rows // tm
    resident = dict(pipeline_mode=pl.Buffered(1))

    def weight_spec(w):
        if isinstance(w, tuple):
            stacked, layer = w
            return stacked, pl.BlockSpec((None,) + stacked.shape[1:], lambda i: (layer, 0, 0), **resident)
        return w, pl.BlockSpec(w.shape, lambda i: (0, 0), **resident)

    (w_gu, gu_spec), (w_d, d_spec) = weight_spec(w_gu), weight_spec(w_d)
    in_specs = [
        pl.BlockSpec((tm, D_MODEL), lambda i: (i, 0)),
        pl.BlockSpec((1, 1, 3 * D_MODEL), lambda i: ((i * tm) // rows_per_batch, 0, 0)),
        pl.BlockSpec((1, D_MODEL), lambda i: (0, 0)),
        gu_spec,
        d_spec,
    ]
    args = [x, mod, norm_g, w_gu, w_d]
    if final_g is not None:
        in_specs.append(pl.BlockSpec((1, D_MODEL), lambda i: (0, 0)))
        args.append(final_g)
    side_in, side_out, side_shapes = _side_cast_io(side, nsteps)
    return pl.pallas_call(
        functools.partial(_ffn_kernel, final_norm=final_g is not None, n_side=len(side)),
        grid=(nsteps,),
        in_specs=in_specs + side_in,
        out_specs=[pl.BlockSpec((tm, D_MODEL), lambda i: (i, 0))] + side_out,
        out_shape=[jax.ShapeDtypeStruct((rows, D_MODEL), F32)] + side_shapes,
        compiler_params=pltpu.CompilerParams(
            dimension_semantics=("arbitrary",), vmem_limit_bytes=V7X_VMEM_LIMIT_BYTES),
        name="ffn",
    )(*args, *[w for w, _ in side])


def _exp2_sum(s, axis):
    e = jnp.exp2(s - jnp.max(s, axis=axis, keepdims=True))
    return e, jnp.sum(e, axis=axis, keepdims=True)


def _mixer_kernel(*refs, seq, nseq, lam_init, rope, layer, n_side):
    (x_ref, mod_ref, g_ref, win_ref, wout_ref, lam_ref, subg_ref,
     convw_ref, ws_ref, cb_ref) = refs[:10]
    refs = refs[10:]
    if rope:
        ck_ref, cv_ref, cos_ref, sin_ref = refs[:4]
        refs = refs[4:]
    elif layer > 0:
        pk_ref, pv_ref = refs[:2]
        refs = refs[2:]
    side_in, o_ref, refs = refs[:n_side], refs[n_side], refs[n_side + 1:]
    if not rope:
        nk_ref, nv_ref = refs[:2]
        refs = refs[2:]
        if layer > 0:
            nk_ref[:, 0:layer] = pk_ref[...]
            nv_ref[:, 0:layer] = pv_ref[...]
        nk_ref, nv_ref = nk_ref.at[:, layer], nv_ref.at[:, layer]
    side_out, refs = refs[:n_side], refs[n_side:]
    _cast_bands(side_in, side_out)
    q1_s, q2_s, k_s, v_s, y_s = refs
    rows = seq * nseq

    x = x_ref[...]
    m = mod_ref[0]
    shift, scale, gate = m[:, 0:D_MODEL], m[:, D_MODEL:2 * D_MODEL], m[:, 2 * D_MODEL:3 * D_MODEL]
    h = _mod_norm(x, g_ref[...], shift, scale).astype(BF16)

    lp = lam_ref[...]
    lam = (jnp.exp(jnp.sum(lp[0:1] * lp[1:2], axis=-1, keepdims=True))
           - jnp.exp(jnp.sum(lp[2:3] * lp[3:4], axis=-1, keepdims=True)) + lam_init)

    lane = lax.broadcasted_iota(jnp.int32, (rows, HEAD_W), 1)
    first_map = lane < ATTN_DH
    if rope:
        first_half = (lane & (ATTN_DH - 1)) < (ATTN_DH // 2)
        cos_t, sin_t = cos_ref[...], sin_ref[...]

        def rotate(t):
            partner = jnp.where(first_half, pltpu.roll(t, HEAD_W - ATTN_DH // 2, 1),
                                pltpu.roll(t, ATTN_DH // 2, 1))
            return t * cos_t + partner * sin_t

    qf = _dot(h, win_ref[:, 0:QKV_W])
    kf = _dot(h, win_ref[:, QKV_W:2 * QKV_W])
    vf = _dot(h, win_ref[:, 2 * QKV_W:3 * QKV_W])
    v_s[0:rows, :] = vf.astype(BF16)
    for hh in range(ATTN_HEADS):
        sl = slice(hh * HEAD_W, (hh + 1) * HEAD_W)
        qh, kh = qf[:, sl], kf[:, sl]
        if rope:
            qh, kh = rotate(qh), rotate(kh)
            k_s[rows:, sl] = ck_ref[hh].astype(BF16)
            v_s[rows:, sl] = cv_ref[hh].astype(BF16)
        else:
            for s in range(nseq):
                nk_ref[s, hh] = kh[s * seq:(s + 1) * seq]
                nv_ref[s, hh] = vf[s * seq:(s + 1) * seq, sl]
        qh = qh * (ATTN_DH ** -0.5 * LOG2_E)
        q1_s[:, sl] = jnp.where(first_map, qh, 0.0).astype(BF16)
        q2_s[:, sl] = jnp.where(first_map, 0.0, qh).astype(BF16)
        k_s[0:rows, sl] = kh.astype(BF16)

    gch = _dot(h, win_ref[:, 3 * QKV_W:3 * QKV_W + 3 * CONV_CH])
    gb, gc, hc = gch[:, 0:CONV_CH], gch[:, CONV_CH:2 * CONV_CH], gch[:, 2 * CONV_CH:3 * CONV_CH]
    xc = gc * hc
    pos = lax.broadcasted_iota(jnp.int32, (rows, CONV_CH), 0) & (seq - 1)
    prev = jnp.where(pos == 0, 0.0, pltpu.roll(xc, 1, 0))
    nxt = jnp.where(pos == seq - 1, 0.0, pltpu.roll(xc, rows - 1, 0))
    cw = convw_ref[...]
    y_s[:, QKV_W:QKV_W + CONV_CH] = (gb * (cw[0:1] * prev + cw[1:2] * xc + cw[2:3] * nxt)).astype(BF16)

    uv = _dot(h, win_ref[:, 3 * QKV_W + 3 * CONV_CH:IN_WIDTH])
    u, vc = uv[:, 0:CHUNK_CH], uv[:, CHUNK_CH:2 * CHUNK_CH]
    grp = lax.broadcasted_iota(jnp.int32, (CHUNK, CHUNK_CH), 1) // CHUNK_GDIM
    for n in range(rows // CHUNK):
        rs = slice(n * CHUNK, (n + 1) * CHUNK)
        vcn = vc[rs]
        sp = cb_ref[...]
        for g in range(CHUNK_GROUPS):
            sp = sp + _dot(ws_ref[g], jnp.where(grp == g, vcn, 0.0).astype(BF16))
        y_s[rs, QKV_W + CONV_CH:D_MODEL] = (u[rs] * sp).astype(BF16)

    kv_len = k_s.shape[0] // nseq
    qblk = min(Q_BLOCK, seq)
    units = [(s, hh) for s in range(nseq) for hh in range(ATTN_HEADS)]
    keys_on_sublanes = qblk * kv_len <= SCORE_ELEMS_ISSUED_AHEAD
    key_axis = 0 if keys_on_sublanes else 1

    def attend(row0):
        def scores(unit):
            s, hh = unit
            sl = slice(hh * HEAD_W, (hh + 1) * HEAD_W)
            kh = k_s[s * kv_len:(s + 1) * kv_len, sl]
            qrows = pl.ds(s * seq + row0, qblk)
            if keys_on_sublanes:
                return _dot_nt(kh, q1_s[qrows, sl]), _dot_nt(kh, q2_s[qrows, sl])
            return _dot_nt(q1_s[qrows, sl], kh), _dot_nt(q2_s[qrows, sl], kh)

        ahead = len(units) if keys_on_sublanes else SCORE_UNITS_AHEAD
        queue = [scores(unit) for unit in units[:ahead]]
        for i, (s, hh) in enumerate(units):
            s1, s2 = queue.pop(0)
            if i + ahead < len(units):
                queue.append(scores(units[i + ahead]))
            sl = slice(hh * HEAD_W, (hh + 1) * HEAD_W)
            vh = v_s[s * kv_len:(s + 1) * kv_len, sl]
            qrows = pl.ds(s * seq + row0, qblk)
            e1, z1 = _exp2_sum(s1, key_axis)
            e2, z2 = _exp2_sum(s2, key_axis)
            if keys_on_sublanes:
                a = (e1 * (1.0 / z1) - e2 * (lam / z2)).astype(BF16)
                o = lax.dot_general(a, vh, (((0,), (0,)), ((), ())), preferred_element_type=F32)
            else:
                a = (e1 - e2 * (lam * z1 / z2)).astype(BF16)
                o = _dot(a, vh) * (1.0 / z1)
            o = (_rms(o) * subg_ref[hh:hh + 1, :]) * (1.0 - lam_init)
            y_s[qrows, sl] = o.astype(BF16)

    nqb = seq // qblk
    if nqb == 1:
        attend(0)
    else:
        def body(qb, carry):
            attend(pl.multiple_of(qb * qblk, qblk))
            return carry
        lax.fori_loop(0, nqb, body, 0)

    o_ref[...] = x + gate * _dot(y_s[...], wout_ref[...])


def _mixer_call(x, mod, norm_g, w_in, w_out, attn_lam, subln_g, conv_w, chunk_ws, chunk_bias,
                layer, seq, nseq, cache=None, prev_kv=None, side=()):
    rows_total = x.shape[0]
    rows = seq * nseq
    nsteps = rows_total // rows
    rope = cache is not None
    lam_init = 0.8 - 0.6 * math.exp(-0.3 * layer)
    resident = dict(pipeline_mode=pl.Buffered(1))
    const2 = lambda i: (0, 0)
    in_specs = [
        pl.BlockSpec((rows, D_MODEL), lambda i: (i, 0)),
        pl.BlockSpec((1, 1, 3 * D_MODEL), lambda i: ((i * rows) // (rows_total // mod.shape[0]), 0, 0)),
        pl.BlockSpec((1, D_MODEL), const2),
        pl.BlockSpec((D_MODEL, IN_WIDTH), const2, **resident),
        pl.BlockSpec((D_MODEL, D_MODEL), const2, **resident),
        pl.BlockSpec((None, 4, ATTN_DH), lambda i: (layer, 0, 0)),
        pl.BlockSpec((None, ATTN_HEADS, HEAD_W), lambda i: (layer, 0, 0)),
        pl.BlockSpec((None, 3, CONV_CH), lambda i: (layer, 0, 0)),
        pl.BlockSpec((None, CHUNK_GROUPS, CHUNK, CHUNK), lambda i: (layer, 0, 0, 0)),
        pl.BlockSpec((None, CHUNK, CHUNK_CH), lambda i: (layer, 0, 0)),
    ]
    args = [x, mod, norm_g, w_in, w_out, attn_lam, subln_g, conv_w, chunk_ws, chunk_bias]
    x_out = jax.ShapeDtypeStruct((rows_total, D_MODEL), F32)
    x_spec = pl.BlockSpec((rows, D_MODEL), lambda i: (i, 0))
    if rope:
        cache_k, cache_v, cos_t, sin_t = cache
        past = cache_k.shape[3]
        kv_spec = pl.BlockSpec((None, None, ATTN_HEADS, past, HEAD_W), lambda i: (i, layer, 0, 0, 0))
        in_specs += [kv_spec, kv_spec,
                     pl.BlockSpec((seq, HEAD_W), const2), pl.BlockSpec((seq, HEAD_W), const2)]
        args += [cache_k, cache_v, cos_t, sin_t]
        out_shape, out_specs = [x_out], [x_spec]
        kv_rows = rows + past
    else:
        kv_rows = rows
        nb = rows_total // seq
        slab = lambda n: pl.BlockSpec((nseq, n, ATTN_HEADS, seq, HEAD_W), lambda i: (i, 0, 0, 0, 0))
        if layer > 0:
            in_specs += [slab(layer), slab(layer)]
            args += list(prev_kv)
        kv_out = jax.ShapeDtypeStruct((nb, layer + 1, ATTN_HEADS, seq, HEAD_W), F32)
        out_shape, out_specs = [x_out, kv_out, kv_out], [x_spec, slab(layer + 1), slab(layer + 1)]
    side_in, side_out, side_shapes = _side_cast_io(side, nsteps)
    args += [w for w, _ in side]
    return pl.pallas_call(
        functools.partial(_mixer_kernel, seq=seq, nseq=nseq, lam_init=lam_init, rope=rope, layer=layer,
                          n_side=len(side)),
        grid=(nsteps,),
        in_specs=in_specs + side_in,
        out_specs=out_specs + side_out,
        out_shape=out_shape + side_shapes,
        scratch_shapes=[pltpu.VMEM((rows, QKV_W), BF16)] * 2 + [pltpu.VMEM((kv_rows, QKV_W), BF16)] * 2
        + [pltpu.VMEM((rows, D_MODEL), BF16)],
        compiler_params=pltpu.CompilerParams(
            dimension_semantics=("arbitrary",), vmem_limit_bytes=V7X_VMEM_LIMIT_BYTES),
        name="mixer_latent" if rope else "mixer_context",
    )(*args)


def _rope_tables(n_tokens):
    n_rows = n_tokens // GRID_W
    row = jnp.repeat(jnp.arange(n_rows, dtype=F32), GRID_W)
    col = jnp.tile(jnp.arange(GRID_W, dtype=F32), n_rows)
    inv = ROPE_BASE ** (-jnp.arange(ROPE_FREQS, dtype=F32) / ROPE_FREQS)
    ang = jnp.concatenate([row[:, None] * inv, col[:, None] * inv], axis=-1)
    cos, sin = jnp.cos(ang), jnp.sin(ang)
    return jnp.tile(cos, (1, 4)), jnp.tile(jnp.concatenate([-sin, sin], axis=-1), (1, 2))


def kernel(x_prompt, x_sample, cache_k, cache_v, c, c_ctx, w_mod, b_mod, norm_g, ffn1_w_gu, ffn1_w_d,
           ffn2_w_gu, ffn2_w_d, w_in, w_out, attn_lam, attn_subln_g, conv_w, chunk_ws, chunk_b,
           final_norm_g):
    nb_ctx, seq_ctx, _ = x_prompt.shape
    nb_lat, seq_lat, _ = x_sample.shape

    cond = jnp.concatenate(
        [c_ctx[None, :], c, jnp.zeros((COND_ROWS - 1 - nb_lat, D_MODEL), F32)], axis=0)
    mods = _mod_call(cond, w_mod, b_mod)

    def mod_rows(layer, sub, stream):
        cols = slice(3 * sub * D_MODEL, 3 * (sub + 1) * D_MODEL)
        r = mods[layer, 0:1, cols] if stream == 0 else mods[layer, 1:1 + nb_lat, cols]
        return r[:, None, :]

    chunk_ws = chunk_ws.astype(BF16)
    chunk_bias = jnp.repeat(jnp.swapaxes(chunk_b, 1, 2), CHUNK_GDIM, axis=2)
    rope = _rope_tables(seq_lat)
    final_g = final_norm_g[None, :]

    xp = x_prompt.reshape(nb_ctx * seq_ctx, D_MODEL)
    xs = x_sample.reshape(nb_lat * seq_lat, D_MODEL)
    new_kv = None
    w1_gu, w1_d = (ffn1_w_gu, 0), (ffn1_w_d, 0)
    for l in range(DEPTH):
        last = l == DEPTH - 1
        ng = lambda k: norm_g[l, k][None, :]
        tm = dict(tm=FFN_TILE_M if l == 0 else 2 * FFN_TILE_M)
        xp, = _ffn_call(xp, mod_rows(l, 0, 0), ng(0), w1_gu, w1_d, nb_ctx * seq_ctx, **tm)
        xs, w_in_l, w_out_l = _ffn_call(xs, mod_rows(l, 0, 1), ng(0), w1_gu, w1_d, seq_lat,
                                        side=[(w_in, l), (w_out, l)], **tm)
        xp, *new_kv, w2_gu, w2_d = _mixer_call(
            xp, mod_rows(l, 1, 0), ng(1), w_in_l, w_out_l, attn_lam, attn_subln_g, conv_w, chunk_ws,
            chunk_bias, l, seq_ctx, CTX_SEQ_PER_STEP, prev_kv=new_kv,
            side=[(ffn2_w_gu, l), (ffn2_w_d, l)])
        xs, = _mixer_call(xs, mod_rows(l, 1, 1), ng(1), w_in_l, w_out_l, attn_lam, attn_subln_g,
                          conv_w, chunk_ws, chunk_bias, l, seq_lat, 1, cache=(cache_k, cache_v) + rope)
        fg = final_g if last else None
        xp, = _ffn_call(xp, mod_rows(l, 2, 0), ng(2), w2_gu, w2_d, nb_ctx * seq_ctx, fg, **tm)
        xs, *next_w1 = _ffn_call(xs, mod_rows(l, 2, 1), ng(2), w2_gu, w2_d, seq_lat, fg,
                                 side=[] if last else [(ffn1_w_gu, l + 1), (ffn1_w_d, l + 1)], **tm)
        if not last:
            w1_gu, w1_d = next_w1

    return (xp.reshape(nb_ctx, seq_ctx, D_MODEL), xs.reshape(nb_lat, seq_lat, D_MODEL), *new_kv)
```

```python
import functools
import math

import jax
import jax.numpy as jnp
from jax import lax
from jax.experimental import pallas as pl
from jax.experimental.pallas import tpu as pltpu

D_MODEL = 1024
D_FF = 2816
DEPTH = 2
N_MOD = 9
GRID_W = 64
ATTN_HEADS = 4
ATTN_DH = 64
HEAD_W = 2 * ATTN_DH
QKV_W = ATTN_HEADS * HEAD_W
CONV_CH = 256
CHUNK = 128
CHUNK_GROUPS = 4
CHUNK_CH = 256
CHUNK_GDIM = CHUNK_CH // CHUNK_GROUPS
IN_WIDTH = 3 * QKV_W + 3 * CONV_CH + 2 * CHUNK_CH
ROPE_BASE = 10000.0
ROPE_FREQS = ATTN_DH // 4
EPS = 1e-6
LOG2_E = math.log2(math.e)

COND_ROWS = 16
MOD_TILE_N = 1152
FFN_TILE_M = 512
FFN_CHUNKS = ((0, 768), (768, 768), (1536, 768), (2304, 512))
FFN_HEAD_SPLIT = 2
MIXER_HEAD_SPLIT = 2
Q_BLOCK = 512
SCORE_ELEMS_ISSUED_AHEAD = 128 * 1024
SCORE_UNITS_AHEAD = 1
CTX_SEQ_PER_STEP = 2
V7X_VMEM_LIMIT_BYTES = 56 * 1024 * 1024

BF16 = jnp.bfloat16
F32 = jnp.float32
BF16_SUBLANES = 16


def _dot(a, b):
    return jnp.dot(a, b, preferred_element_type=F32)


def _dot_nt(a, b):
    return lax.dot_general(a, b, (((1,), (1,)), ((), ())), preferred_element_type=F32)


def _rms(x):
    return x * lax.rsqrt(jnp.mean(x * x, axis=-1, keepdims=True) + EPS)


def _mod_norm(x, g, shift, scale):
    return _rms(x) * (g * (1.0 + scale)) + shift


def _mod_kernel(c_ref, w_ref, b_ref, o_ref):
    a = jax.nn.silu(c_ref[...]).astype(BF16)
    o_ref[...] = _dot(a, w_ref[...].astype(BF16)) + b_ref[...]


def _mod_call(cond, w_mod, b_mod):
    n = N_MOD * D_MODEL
    return pl.pallas_call(
        _mod_kernel,
        grid=(DEPTH, n // MOD_TILE_N),
        in_specs=[
            pl.BlockSpec((COND_ROWS, D_MODEL), lambda l, j: (0, 0)),
            pl.BlockSpec((None, D_MODEL, MOD_TILE_N), lambda l, j: (l, 0, j)),
            pl.BlockSpec((None, 1, MOD_TILE_N), lambda l, j: (l, 0, j)),
        ],
        out_specs=pl.BlockSpec((None, COND_ROWS, MOD_TILE_N), lambda l, j: (l, 0, j)),
        out_shape=jax.ShapeDtypeStruct((DEPTH, COND_ROWS, n), F32),
        compiler_params=pltpu.CompilerParams(dimension_semantics=("arbitrary", "arbitrary")),
        name="mod_rows",
    )(cond, w_mod, b_mod.reshape(DEPTH, 1, n))


def _cast_bands(src_refs, dst_refs):
    for src, dst in zip(src_refs, dst_refs):
        dst[...] = src[...].astype(BF16)


def _side_cast_io(side, nsteps):
    in_specs, out_specs, out_shapes = [], [], []
    for w, layer in side:
        _, r, c = w.shape
        band = r // nsteps
        assert band * nsteps == r and band % BF16_SUBLANES == 0, (w.shape, nsteps)
        in_specs.append(pl.BlockSpec((None, band, c), lambda i, layer=layer: (layer, i, 0)))
        out_specs.append(pl.BlockSpec((band, c), lambda i: (i, 0)))
        out_shapes.append(jax.ShapeDtypeStruct((r, c), BF16))
    return in_specs, out_specs, out_shapes


def _ffn_kernel(x_ref, mod_ref, g_ref, wgu_ref, wd_ref, *rest, final_norm, n_side):
    if final_norm:
        fg_ref, rest = rest[0], rest[1:]
    side_in, o_ref, side_out = rest[:n_side], rest[n_side], rest[n_side + 1:]
    _cast_bands(side_in, side_out)
    x = x_ref[...]
    m = mod_ref[0]
    shift, scale, gate = m[:, 0:D_MODEL], m[:, D_MODEL:2 * D_MODEL], m[:, 2 * D_MODEL:3 * D_MODEL]
    rb = x.shape[0] // FFN_HEAD_SPLIT
    h_blocks = [_mod_norm(x[r * rb:(r + 1) * rb], g_ref[...], shift, scale).astype(BF16)
                for r in range(FFN_HEAD_SPLIT)]
    h = jnp.concatenate(h_blocks, axis=0)
    acc = None
    for c0, cw in FFN_CHUNKS:
        w_g = wgu_ref[:, c0:c0 + cw].astype(BF16)
        w_u = wgu_ref[:, D_FF + c0:D_FF + c0 + cw].astype(BF16)
        if c0 == 0:
            g = jnp.concatenate([_dot(hb, w_g) for hb in h_blocks], axis=0)
            u = jnp.concatenate([_dot(hb, w_u) for hb in h_blocks], axis=0)
        else:
            g, u = _dot(h, w_g), _dot(h, w_u)
        a = (jax.nn.silu(g) * u).astype(BF16)
        part = _dot(a, wd_ref[c0:c0 + cw, :].astype(BF16))
        acc = part if acc is None else acc + part
    y = x + 0.5 * gate * acc
    if final_norm:
        y = _rms(y) * fg_ref[...]
    o_ref[...] = y


def _ffn_call(x, mod, norm_g, w_gu, w_d, rows_per_batch, final_g=None, side=(), tm=FFN_TILE_M):
    rows = x.shape[0]
    nsteps = rows // tm
    resident = dict(pipeline_mode=pl.Buffered(1))

    def weight_spec(w):
        if isinstance(w, tuple):
            stacked, layer = w
            return stacked, pl.BlockSpec((None,) + stacked.shape[1:], lambda i: (layer, 0, 0), **resident)
        return w, pl.BlockSpec(w.shape, lambda i: (0, 0), **resident)

    (w_gu, gu_spec), (w_d, d_spec) = weight_spec(w_gu), weight_spec(w_d)
    in_specs = [
        pl.BlockSpec((tm, D_MODEL), lambda i: (i, 0)),
        pl.BlockSpec((1, 1, 3 * D_MODEL), lambda i: ((i * tm) // rows_per_batch, 0, 0)),
        pl.BlockSpec((1, D_MODEL), lambda i: (0, 0)),
        gu_spec,
        d_spec,
    ]
    args = [x, mod, norm_g, w_gu, w_d]
    if final_g is not None:
        in_specs.append(pl.BlockSpec((1, D_MODEL), lambda i: (0, 0)))
        args.append(final_g)
    side_in, side_out, side_shapes = _side_cast_io(side, nsteps)
    return pl.pallas_call(
        functools.partial(_ffn_kernel, final_norm=final_g is not None, n_side=len(side)),
        grid=(nsteps,),
        in_specs=in_specs + side_in,
        out_specs=[pl.BlockSpec((tm, D_MODEL), lambda i: (i, 0))] + side_out,
        out_shape=[jax.ShapeDtypeStruct((rows, D_MODEL), F32)] + side_shapes,
        compiler_params=pltpu.CompilerParams(
            dimension_semantics=("arbitrary",), vmem_limit_bytes=V7X_VMEM_LIMIT_BYTES),
        name="ffn",
    )(*args, *[w for w, _ in side])


def _exp2_sum(s, axis):
    e = jnp.exp2(s - jnp.max(s, axis=axis, keepdims=True))
    return e, jnp.sum(e, axis=axis, keepdims=True)


def _mixer_kernel(*refs, seq, nseq, lam_init, rope, layer, n_side):
    (x_ref, mod_ref, g_ref, win_ref, wout_ref, lam_ref, subg_ref,
     convw_ref, ws_ref, cb_ref) = refs[:10]
    refs = refs[10:]
    if rope:
        ck_ref, cv_ref, cos_ref, sin_ref = refs[:4]
        refs = refs[4:]
    elif layer > 0:
        pk_ref, pv_ref = refs[:2]
        refs = refs[2:]
    side_in, o_ref, refs = refs[:n_side], refs[n_side], refs[n_side + 1:]
    if not rope:
        nk_ref, nv_ref = refs[:2]
        refs = refs[2:]
        if layer > 0:
            nk_ref[:, 0:layer] = pk_ref[...]
            nv_ref[:, 0:layer] = pv_ref[...]
        nk_ref, nv_ref = nk_ref.at[:, layer], nv_ref.at[:, layer]
    side_out, refs = refs[:n_side], refs[n_side:]
    _cast_bands(side_in, side_out)
    q1_s, q2_s, k_s, v_s, y_s = refs
    rows = seq * nseq

    x = x_ref[...]
    m = mod_ref[0]
    shift, scale, gate = m[:, 0:D_MODEL], m[:, D_MODEL:2 * D_MODEL], m[:, 2 * D_MODEL:3 * D_MODEL]
    rb = rows // MIXER_HEAD_SPLIT
    h_blocks = [_mod_norm(x[r * rb:(r + 1) * rb], g_ref[...], shift, scale).astype(BF16)
                for r in range(MIXER_HEAD_SPLIT)]
    h = jnp.concatenate(h_blocks, axis=0)

    lp = lam_ref[...]
    lam = (jnp.exp(jnp.sum(lp[0:1] * lp[1:2], axis=-1, keepdims=True))
           - jnp.exp(jnp.sum(lp[2:3] * lp[3:4], axis=-1, keepdims=True)) + lam_init)

    lane = lax.broadcasted_iota(jnp.int32, (rows, HEAD_W), 1)
    first_map = lane < ATTN_DH
    if rope:
        first_half = (lane & (ATTN_DH - 1)) < (ATTN_DH // 2)
        cos_t, sin_t = cos_ref[...], sin_ref[...]

        def rotate(t):
            partner = jnp.where(first_half, pltpu.roll(t, HEAD_W - ATTN_DH // 2, 1),
                                pltpu.roll(t, ATTN_DH // 2, 1))
            return t * cos_t + partner * sin_t

    qf = jnp.concatenate([_dot(hb, win_ref[:, 0:QKV_W]) for hb in h_blocks], axis=0)
    kf = _dot(h, win_ref[:, QKV_W:2 * QKV_W])
    vf = _dot(h, win_ref[:, 2 * QKV_W:3 * QKV_W])
    v_s[0:rows, :] = vf.astype(BF16)
    for hh in range(ATTN_HEADS):
        sl = slice(hh * HEAD_W, (hh + 1) * HEAD_W)
        qh, kh = qf[:, sl], kf[:, sl]
        if rope:
            qh, kh = rotate(qh), rotate(kh)
            k_s[rows:, sl] = ck_ref[hh].astype(BF16)
            v_s[rows:, sl] = cv_ref[hh].astype(BF16)
        else:
            for s in range(nseq):
                nk_ref[s, hh] = kh[s * seq:(s + 1) * seq]
                nv_ref[s, hh] = vf[s * seq:(s + 1) * seq, sl]
        qh = qh * (ATTN_DH ** -0.5 * LOG2_E)
        q1_s[:, sl] = jnp.where(first_map, qh, 0.0).astype(BF16)
        q2_s[:, sl] = jnp.where(first_map, 0.0, qh).astype(BF16)
        k_s[0:rows, sl] = kh.astype(BF16)

    gch = _dot(h, win_ref[:, 3 * QKV_W:3 * QKV_W + 3 * CONV_CH])
    gb, gc, hc = gch[:, 0:CONV_CH], gch[:, CONV_CH:2 * CONV_CH], gch[:, 2 * CONV_CH:3 * CONV_CH]
    xc = gc * hc
    pos = lax.broadcasted_iota(jnp.int32, (rows, CONV_CH), 0) & (seq - 1)
    prev = jnp.where(pos == 0, 0.0, pltpu.roll(xc, 1, 0))
    nxt = jnp.where(pos == seq - 1, 0.0, pltpu.roll(xc, rows - 1, 0))
    cw = convw_ref[...]
    y_s[:, QKV_W:QKV_W + CONV_CH] = (gb * (cw[0:1] * prev + cw[1:2] * xc + cw[2:3] * nxt)).astype(BF16)

    kv_len = k_s.shape[0] // nseq
    qblk = min(Q_BLOCK, seq)
    units = [(s, hh) for s in range(nseq) for hh in range(ATTN_HEADS)]
    keys_on_sublanes = qblk * kv_len <= SCORE_ELEMS_ISSUED_AHEAD
    key_axis = 0 if keys_on_sublanes else 1

    def attend(row0):
        def scores(unit):
            s, hh = unit
            sl = slice(hh * HEAD_W, (hh + 1) * HEAD_W)
            kh = k_s[s * kv_len:(s + 1) * kv_len, sl]
            qrows = pl.ds(s * seq + row0, qblk)
            if keys_on_sublanes:
                return _dot_nt(kh, q1_s[qrows, sl]), _dot_nt(kh, q2_s[qrows, sl])
            return _dot_nt(q1_s[qrows, sl], kh), _dot_nt(q2_s[qrows, sl], kh)

        ahead = len(units) if keys_on_sublanes else SCORE_UNITS_AHEAD
        queue = [scores(unit) for unit in units[:ahead]]
        for i, (s, hh) in enumerate(units):
            s1, s2 = queue.pop(0)
            if i + ahead < len(units):
                queue.append(scores(units[i + ahead]))
            sl = slice(hh * HEAD_W, (hh + 1) * HEAD_W)
            vh = v_s[s * kv_len:(s + 1) * kv_len, sl]
            qrows = pl.ds(s * seq + row0, qblk)
            e1, z1 = _exp2_sum(s1, key_axis)
            e2, z2 = _exp2_sum(s2, key_axis)
            if keys_on_sublanes:
                a = (e1 * (1.0 / z1) - e2 * (lam / z2)).astype(BF16)
                o = lax.dot_general(a, vh, (((0,), (0,)), ((), ())), preferred_element_type=F32)
            else:
                a = (e1 - e2 * (lam * z1 / z2)).astype(BF16)
                o = _dot(a, vh) * (1.0 / z1)
            o = (_rms(o) * subg_ref[hh:hh + 1, :]) * (1.0 - lam_init)
            y_s[qrows, sl] = o.astype(BF16)

    nqb = seq // qblk
    if nqb == 1:
        attend(0)
    else:
        def body(qb, carry):
            attend(pl.multiple_of(qb * qblk, qblk))
            return carry
        lax.fori_loop(0, nqb, body, 0)

    uv = _dot(h, win_ref[:, 3 * QKV_W + 3 * CONV_CH:IN_WIDTH])
    u, vc = uv[:, 0:CHUNK_CH], uv[:, CHUNK_CH:2 * CHUNK_CH]
    grp = lax.broadcasted_iota(jnp.int32, (CHUNK, CHUNK_CH), 1) // CHUNK_GDIM
    y_chunk = []
    for n in range(rows // CHUNK):
        rs = slice(n * CHUNK, (n + 1) * CHUNK)
        vcn = vc[rs]
        sp = cb_ref[...]
        for g in range(CHUNK_GROUPS):
            sp = sp + _dot(ws_ref[g], jnp.where(grp == g, vcn, 0.0).astype(BF16))
        y_chunk.append((u[rs] * sp).astype(BF16))

    split = QKV_W + CONV_CH
    y = _dot(y_s[:, 0:split], wout_ref[0:split, :])
    y = y + _dot(jnp.concatenate(y_chunk, axis=0), wout_ref[split:, :])
    o_ref[...] = x + gate * y


def _mixer_call(x, mod, norm_g, w_in, w_out, attn_lam, subln_g, conv_w, chunk_ws, chunk_bias,
                layer, seq, nseq, cache=None, prev_kv=None, side=()):
    rows_total = x.shape[0]
    rows = seq * nseq
    nsteps = rows_total // rows
    rope = cache is not None
    lam_init = 0.8 - 0.6 * math.exp(-0.3 * layer)
    resident = dict(pipeline_mode=pl.Buffered(1))
    const2 = lambda i: (0, 0)
    in_specs = [
        pl.BlockSpec((rows, D_MODEL), lambda i: (i, 0)),
        pl.BlockSpec((1, 1, 3 * D_MODEL), lambda i: ((i * rows) // (rows_total // mod.shape[0]), 0, 0)),
        pl.BlockSpec((1, D_MODEL), const2),
        pl.BlockSpec((D_MODEL, IN_WIDTH), const2, **resident),
        pl.BlockSpec((D_MODEL, D_MODEL), const2, **resident),
        pl.BlockSpec((None, 4, ATTN_DH), lambda i: (layer, 0, 0)),
        pl.BlockSpec((None, ATTN_HEADS, HEAD_W), lambda i: (layer, 0, 0)),
        pl.BlockSpec((None, 3, CONV_CH), lambda i: (layer, 0, 0)),
        pl.BlockSpec((None, CHUNK_GROUPS, CHUNK, CHUNK), lambda i: (layer, 0, 0, 0)),
        pl.BlockSpec((None, CHUNK, CHUNK_CH), lambda i: (layer, 0, 0)),
    ]
    args = [x, mod, norm_g, w_in, w_out, attn_lam, subln_g, conv_w, chunk_ws, chunk_bias]
    x_out = jax.ShapeDtypeStruct((rows_total, D_MODEL), F32)
    x_spec = pl.BlockSpec((rows, D_MODEL), lambda i: (i, 0))
    if rope:
        cache_k, cache_v, cos_t, sin_t = cache
        past = cache_k.shape[3]
        kv_spec = pl.BlockSpec((None, None, ATTN_HEADS, past, HEAD_W), lambda i: (i, layer, 0, 0, 0))
        in_specs += [kv_spec, kv_spec,
                     pl.BlockSpec((seq, HEAD_W), const2), pl.BlockSpec((seq, HEAD_W), const2)]
        args += [cache_k, cache_v, cos_t, sin_t]
        out_shape, out_specs = [x_out], [x_spec]
        kv_rows = rows + past
    else:
        kv_rows = rows
        nb = rows_total // seq
        slab = lambda n: pl.BlockSpec((nseq, n, ATTN_HEADS, seq, HEAD_W), lambda i: (i, 0, 0, 0, 0))
        if layer > 0:
            in_specs += [slab(layer), slab(layer)]
            args += list(prev_kv)
        kv_out = jax.ShapeDtypeStruct((nb, layer + 1, ATTN_HEADS, seq, HEAD_W), F32)
        out_shape, out_specs = [x_out, kv_out, kv_out], [x_spec, slab(layer + 1), slab(layer + 1)]
    side_in, side_out, side_shapes = _side_cast_io(side, nsteps)
    args += [w for w, _ in side]
    return pl.pallas_call(
        functools.partial(_mixer_kernel, seq=seq, nseq=nseq, lam_init=lam_init, rope=rope, layer=layer,
                          n_side=len(side)),
        grid=(nsteps,),
        in_specs=in_specs + side_in,
        out_specs=out_specs + side_out,
        out_shape=out_shape + side_shapes,
        scratch_shapes=[pltpu.VMEM((rows, QKV_W), BF16)] * 2 + [pltpu.VMEM((kv_rows, QKV_W), BF16)] * 2
        + [pltpu.VMEM((rows, D_MODEL), BF16)],
        compiler_params=pltpu.CompilerParams(
            dimension_semantics=("arbitrary",), vmem_limit_bytes=V7X_VMEM_LIMIT_BYTES),
        name="mixer_latent" if rope else "mixer_context",
    )(*args)


def _rope_tables(n_tokens):
    n_rows = n_tokens // GRID_W
    row = jnp.repeat(jnp.arange(n_rows, dtype=F32), GRID_W)
    col = jnp.tile(jnp.arange(GRID_W, dtype=F32), n_rows)
    inv = ROPE_BASE ** (-jnp.arange(ROPE_FREQS, dtype=F32) / ROPE_FREQS)
    ang = jnp.concatenate([row[:, None] * inv, col[:, None] * inv], axis=-1)
    cos, sin = jnp.cos(ang), jnp.sin(ang)
    return jnp.tile(cos, (1, 4)), jnp.tile(jnp.concatenate([-sin, sin], axis=-1), (1, 2))


def kernel(x_prompt, x_sample, cache_k, cache_v, c, c_ctx, w_mod, b_mod, norm_g, ffn1_w_gu, ffn1_w_d,
           ffn2_w_gu, ffn2_w_d, w_in, w_out, attn_lam, attn_subln_g, conv_w, chunk_ws, chunk_b,
           final_norm_g):
    nb_ctx, seq_ctx, _ = x_prompt.shape
    nb_lat, seq_lat, _ = x_sample.shape

    cond = jnp.concatenate(
        [c_ctx[None, :], c, jnp.zeros((COND_ROWS - 1 - nb_lat, D_MODEL), F32)], axis=0)
    mods = _mod_call(cond, w_mod, b_mod)

    def mod_rows(layer, sub, stream):
        cols = slice(3 * sub * D_MODEL, 3 * (sub + 1) * D_MODEL)
        r = mods[layer, 0:1, cols] if stream == 0 else mods[layer, 1:1 + nb_lat, cols]
        return r[:, None, :]

    chunk_ws = chunk_ws.astype(BF16)
    chunk_bias = jnp.repeat(jnp.swapaxes(chunk_b, 1, 2), CHUNK_GDIM, axis=2)
    rope = _rope_tables(seq_lat)
    final_g = final_norm_g[None, :]

    xp = x_prompt.reshape(nb_ctx * seq_ctx, D_MODEL)
    xs = x_sample.reshape(nb_lat * seq_lat, D_MODEL)
    new_kv = None
    w1_gu, w1_d = (ffn1_w_gu, 0), (ffn1_w_d, 0)
    for l in range(DEPTH):
        last = l == DEPTH - 1
        ng = lambda k: norm_g[l, k][None, :]
        tm = dict(tm=FFN_TILE_M)
        xp, = _ffn_call(xp, mod_rows(l, 0, 0), ng(0), w1_gu, w1_d, nb_ctx * seq_ctx, **tm)
        xs, w_in_l, w_out_l = _ffn_call(xs, mod_rows(l, 0, 1), ng(0), w1_gu, w1_d, seq_lat,
                                        side=[(w_in, l), (w_out, l)], **tm)
        xp, *new_kv, w2_gu, w2_d = _mixer_call(
            xp, mod_rows(l, 1, 0), ng(1), w_in_l, w_out_l, attn_lam, attn_subln_g, conv_w, chunk_ws,
            chunk_bias, l, seq_ctx, CTX_SEQ_PER_STEP, prev_kv=new_kv,
            side=[(ffn2_w_gu, l), (ffn2_w_d, l)])
        xs, = _mixer_call(xs, mod_rows(l, 1, 1), ng(1), w_in_l, w_out_l, attn_lam, attn_subln_g,
                          conv_w, chunk_ws, chunk_bias, l, seq_lat, 1, cache=(cache_k, cache_v) + rope)
        fg = final_g if last else None
        xp, = _ffn_call(xp, mod_rows(l, 2, 0), ng(2), w2_gu, w2_d, nb_ctx * seq_ctx, fg, **tm)
        xs, *next_w1 = _ffn_call(xs, mod_rows(l, 2, 1), ng(2), w2_gu, w2_d, seq_lat, fg,
                                 side=[] if last else [(ffn1_w_gu, l + 1), (ffn1_w_d, l + 1)], **tm)
        if not last:
            w1_gu, w1_d = next_w1

    return (xp.reshape(nb_ctx, seq_ctx, D_MODEL), xs.reshape(nb_lat, seq_lat, D_MODEL), *new_kv)
```

```python
import functools
import math

import jax
import jax.numpy as jnp
import numpy as np
from jax import lax
from jax.experimental import pallas as pl
from jax.experimental.pallas import tpu as pltpu

D_MODEL = 1024
D_FF = 2816
DEPTH = 2
N_MOD = 9
GRID_W = 64
ATTN_HEADS = 4
ATTN_DH = 64
HEAD_W = 2 * ATTN_DH
QKV_W = ATTN_HEADS * HEAD_W
CONV_CH = 256
CHUNK = 128
CHUNK_GROUPS = 4
CHUNK_CH = 256
CHUNK_GDIM = CHUNK_CH // CHUNK_GROUPS
IN_WIDTH = 3 * QKV_W + 3 * CONV_CH + 2 * CHUNK_CH
ROPE_BASE = 10000.0
ROPE_FREQS = ATTN_DH // 4
EPS = 1e-6
LOG2_E = math.log2(math.e)

COND_ROWS = 16
MOD_TILE_N = 2304
FFN_TILE_M = 512
FFN_CHUNKS = ((0, 768), (768, 768), (1536, 768), (2304, 512))
FFN_HEAD_SPLIT = 2
MIXER_HEAD_SPLIT = 2
Q_BLOCK = 512
SCORE_ELEMS_ISSUED_AHEAD = 128 * 1024
SCORE_UNITS_AHEAD = 1
CTX_SEQ_PER_STEP = 2
V7X_VMEM_LIMIT_BYTES = 56 * 1024 * 1024

BF16 = jnp.bfloat16
F32 = jnp.float32
BF16_SUBLANES = 16


def _dot(a, b):
    return jnp.dot(a, b, preferred_element_type=F32)


def _dot_nt(a, b):
    return lax.dot_general(a, b, (((1,), (1,)), ((), ())), preferred_element_type=F32)


def _rms(x):
    return x * lax.rsqrt(jnp.mean(x * x, axis=-1, keepdims=True) + EPS)


def _mod_norm(x, g, shift, scale):
    return _rms(x) * (g * (1.0 + scale)) + shift


def _mod_parts(mod_ref):
    m = mod_ref[...]
    return m[:, 0:D_MODEL], m[:, D_MODEL:2 * D_MODEL], m[:, 2 * D_MODEL:3 * D_MODEL]


def _mod_spec(layer, sub, row0, rows_per_step, rows_per_batch):
    return pl.BlockSpec((None, None, 1, 3 * D_MODEL),
                        lambda i: (layer, row0 + (i * rows_per_step) // rows_per_batch, 0, sub))


def _mod_kernel(c_ref, w_ref, b_ref, o_ref):
    a = jax.nn.silu(c_ref[...]).astype(BF16)
    m = _dot(a, w_ref[...].astype(BF16)) + b_ref[...]
    for r in range(COND_ROWS):
        o_ref[r] = m[r:r + 1, :]


def _mod_call(cond, w_mod, b_mod):
    n = N_MOD * D_MODEL
    return pl.pallas_call(
        _mod_kernel,
        grid=(DEPTH, n // MOD_TILE_N),
        in_specs=[
            pl.BlockSpec((COND_ROWS, D_MODEL), lambda l, j: (0, 0)),
            pl.BlockSpec((None, D_MODEL, MOD_TILE_N), lambda l, j: (l, 0, j)),
            pl.BlockSpec((None, 1, MOD_TILE_N), lambda l, j: (l, 0, j)),
        ],
        out_specs=pl.BlockSpec((None, COND_ROWS, 1, MOD_TILE_N), lambda l, j: (l, 0, 0, j)),
        out_shape=jax.ShapeDtypeStruct((DEPTH, COND_ROWS, 1, n), F32),
        compiler_params=pltpu.CompilerParams(dimension_semantics=("arbitrary", "arbitrary")),
        name="mod_rows",
    )(cond, w_mod, b_mod.reshape(DEPTH, 1, n))


def _cast_bands(src_refs, dst_refs):
    for src, dst in zip(src_refs, dst_refs):
        dst[...] = src[...].astype(BF16)


def _side_cast_io(side, nsteps):
    in_specs, out_specs, out_shapes = [], [], []
    for w, layer in side:
        _, r, c = w.shape
        band = r // nsteps
        assert band * nsteps == r and band % BF16_SUBLANES == 0, (w.shape, nsteps)
        in_specs.append(pl.BlockSpec((None, band, c), lambda i, layer=layer: (layer, i, 0)))
        out_specs.append(pl.BlockSpec((band, c), lambda i: (i, 0)))
        out_shapes.append(jax.ShapeDtypeStruct((r, c), BF16))
    return in_specs, out_specs, out_shapes


def _ffn_kernel(x_ref, mod_ref, g_ref, wgu_ref, wd_ref, *rest, norm_row, final_norm, n_side):
    if final_norm:
        fg_ref, rest = rest[0], rest[1:]
    side_in, o_ref, side_out = rest[:n_side], rest[n_side], rest[n_side + 1:]
    _cast_bands(side_in, side_out)
    x = x_ref[...]
    shift, scale, gate = _mod_parts(mod_ref)
    g_norm = g_ref[norm_row:norm_row + 1, :]
    rb = x.shape[0] // FFN_HEAD_SPLIT
    h_blocks = [_mod_norm(x[r * rb:(r + 1) * rb], g_norm, shift, scale).astype(BF16)
                for r in range(FFN_HEAD_SPLIT)]
    h = jnp.concatenate(h_blocks, axis=0)
    acc = None
    for c0, cw in FFN_CHUNKS:
        w_g = wgu_ref[:, c0:c0 + cw].astype(BF16)
        w_u = wgu_ref[:, D_FF + c0:D_FF + c0 + cw].astype(BF16)
        if c0 == 0:
            g = jnp.concatenate([_dot(hb, w_g) for hb in h_blocks], axis=0)
            u = jnp.concatenate([_dot(hb, w_u) for hb in h_blocks], axis=0)
        else:
            g, u = _dot(h, w_g), _dot(h, w_u)
        a = (jax.nn.silu(g) * u).astype(BF16)
        part = _dot(a, wd_ref[c0:c0 + cw, :].astype(BF16))
        acc = part if acc is None else acc + part
    y = x + 0.5 * gate * acc
    if final_norm:
        y = _rms(y) * fg_ref[...]
    o_ref[...] = y


def _ffn_call(x, mods, norm_g, w_gu, w_d, layer, sub, mod_row0, rows_per_batch, final_g=None, side=()):
    rows = x.shape[0]
    tm = FFN_TILE_M
    nsteps = rows // tm
    resident = dict(pipeline_mode=pl.Buffered(1))

    def weight_spec(w):
        if isinstance(w, tuple):
            stacked, w_layer = w
            return stacked, pl.BlockSpec((None,) + stacked.shape[1:], lambda i: (w_layer, 0, 0), **resident)
        return w, pl.BlockSpec(w.shape, lambda i: (0, 0), **resident)

    (w_gu, gu_spec), (w_d, d_spec) = weight_spec(w_gu), weight_spec(w_d)
    in_specs = [
        pl.BlockSpec((tm, D_MODEL), lambda i: (i, 0)),
        _mod_spec(layer, sub, mod_row0, tm, rows_per_batch),
        pl.BlockSpec((None, 3, D_MODEL), lambda i: (layer, 0, 0)),
        gu_spec,
        d_spec,
    ]
    args = [x, mods, norm_g, w_gu, w_d]
    if final_g is not None:
        in_specs.append(pl.BlockSpec((1, D_MODEL), lambda i: (0, 0)))
        args.append(final_g)
    side_in, side_out, side_shapes = _side_cast_io(side, nsteps)
    return pl.pallas_call(
        functools.partial(_ffn_kernel, norm_row=sub, final_norm=final_g is not None, n_side=len(side)),
        grid=(nsteps,),
        in_specs=in_specs + side_in,
        out_specs=[pl.BlockSpec((tm, D_MODEL), lambda i: (i, 0))] + side_out,
        out_shape=[jax.ShapeDtypeStruct((rows, D_MODEL), F32)] + side_shapes,
        compiler_params=pltpu.CompilerParams(
            dimension_semantics=("arbitrary",), vmem_limit_bytes=V7X_VMEM_LIMIT_BYTES),
        name="ffn",
    )(*args, *[w for w, _ in side])


def _exp2_sum(s, axis):
    e = jnp.exp2(s - jnp.max(s, axis=axis, keepdims=True))
    return e, jnp.sum(e, axis=axis, keepdims=True)


def _mixer_kernel(*refs, seq, nseq, lam_init, rope, layer, n_side):
    (x_ref, mod_ref, g_ref, win_ref, wout_ref, lam_ref, subg_ref,
     convw_ref, ws_ref, cb_ref) = refs[:10]
    refs = refs[10:]
    if rope:
        ck_ref, cv_ref, cos_ref, sin_ref = refs[:4]
        refs = refs[4:]
    elif layer > 0:
        pk_ref, pv_ref = refs[:2]
        refs = refs[2:]
    side_in, o_ref, refs = refs[:n_side], refs[n_side], refs[n_side + 1:]
    if not rope:
        nk_ref, nv_ref = refs[:2]
        refs = refs[2:]
        if layer > 0:
            nk_ref[:, 0:layer] = pk_ref[...]
            nv_ref[:, 0:layer] = pv_ref[...]
        nk_ref, nv_ref = nk_ref.at[:, layer], nv_ref.at[:, layer]
    side_out, refs = refs[:n_side], refs[n_side:]
    _cast_bands(side_in, side_out)
    q1_s, q2_s, k_s, v_s, y_s = refs
    rows = seq * nseq

    x = x_ref[...]
    shift, scale, gate = _mod_parts(mod_ref)
    g_norm = g_ref[1:2, :]
    rb = rows // MIXER_HEAD_SPLIT
    h_blocks = [_mod_norm(x[r * rb:(r + 1) * rb], g_norm, shift, scale).astype(BF16)
                for r in range(MIXER_HEAD_SPLIT)]
    h = jnp.concatenate(h_blocks, axis=0)

    lp = lam_ref[...]
    lam = (jnp.exp(jnp.sum(lp[0:1] * lp[1:2], axis=-1, keepdims=True))
           - jnp.exp(jnp.sum(lp[2:3] * lp[3:4], axis=-1, keepdims=True)) + lam_init)

    lane = lax.broadcasted_iota(jnp.int32, (rows, HEAD_W), 1)
    first_map = lane < ATTN_DH
    if rope:
        first_half = (lane & (ATTN_DH - 1)) < (ATTN_DH // 2)
        cos_t, sin_t = cos_ref[...], sin_ref[...]

        def rotate(t):
            partner = jnp.where(first_half, pltpu.roll(t, HEAD_W - ATTN_DH // 2, 1),
                                pltpu.roll(t, ATTN_DH // 2, 1))
            return t * cos_t + partner * sin_t

    qf = jnp.concatenate([_dot(hb, win_ref[:, 0:QKV_W]) for hb in h_blocks], axis=0)
    kf = _dot(h, win_ref[:, QKV_W:2 * QKV_W])
    vf = _dot(h, win_ref[:, 2 * QKV_W:3 * QKV_W])
    v_s[0:rows, :] = vf.astype(BF16)
    for hh in range(ATTN_HEADS):
        sl = slice(hh * HEAD_W, (hh + 1) * HEAD_W)
        qh, kh = qf[:, sl], kf[:, sl]
        if rope:
            qh, kh = rotate(qh), rotate(kh)
            k_s[rows:, sl] = ck_ref[hh].astype(BF16)
            v_s[rows:, sl] = cv_ref[hh].astype(BF16)
        else:
            for s in range(nseq):
                nk_ref[s, hh] = kh[s * seq:(s + 1) * seq]
                nv_ref[s, hh] = vf[s * seq:(s + 1) * seq, sl]
        qh = qh * (ATTN_DH ** -0.5 * LOG2_E)
        q1_s[:, sl] = jnp.where(first_map, qh, 0.0).astype(BF16)
        q2_s[:, sl] = jnp.where(first_map, 0.0, qh).astype(BF16)
        k_s[0:rows, sl] = kh.astype(BF16)

    gch = _dot(h, win_ref[:, 3 * QKV_W:3 * QKV_W + 3 * CONV_CH])
    gb, gc, hc = gch[:, 0:CONV_CH], gch[:, CONV_CH:2 * CONV_CH], gch[:, 2 * CONV_CH:3 * CONV_CH]
    xc = gc * hc
    pos = lax.broadcasted_iota(jnp.int32, (rows, CONV_CH), 0) & (seq - 1)
    prev = jnp.where(pos == 0, 0.0, pltpu.roll(xc, 1, 0))
    nxt = jnp.where(pos == seq - 1, 0.0, pltpu.roll(xc, rows - 1, 0))
    cw = convw_ref[...]
    y_s[:, QKV_W:QKV_W + CONV_CH] = (gb * (cw[0:1] * prev + cw[1:2] * xc + cw[2:3] * nxt)).astype(BF16)

    kv_len = k_s.shape[0] // nseq
    qblk = min(Q_BLOCK, seq)
    units = [(s, hh) for s in range(nseq) for hh in range(ATTN_HEADS)]
    keys_on_sublanes = qblk * kv_len <= SCORE_ELEMS_ISSUED_AHEAD
    key_axis = 0 if keys_on_sublanes else 1

    def attend(row0):
        def scores(unit):
            s, hh = unit
            sl = slice(hh * HEAD_W, (hh + 1) * HEAD_W)
            kh = k_s[s * kv_len:(s + 1) * kv_len, sl]
            qrows = pl.ds(s * seq + row0, qblk)
            if keys_on_sublanes:
                return _dot_nt(kh, q1_s[qrows, sl]), _dot_nt(kh, q2_s[qrows, sl])
            return _dot_nt(q1_s[qrows, sl], kh), _dot_nt(q2_s[qrows, sl], kh)

        ahead = len(units) if keys_on_sublanes else SCORE_UNITS_AHEAD
        queue = [scores(unit) for unit in units[:ahead]]
        for i, (s, hh) in enumerate(units):
            s1, s2 = queue.pop(0)
            if i + ahead < len(units):
                queue.append(scores(units[i + ahead]))
            sl = slice(hh * HEAD_W, (hh + 1) * HEAD_W)
            vh = v_s[s * kv_len:(s + 1) * kv_len, sl]
            qrows = pl.ds(s * seq + row0, qblk)
            e1, z1 = _exp2_sum(s1, key_axis)
            e2, z2 = _exp2_sum(s2, key_axis)
            if keys_on_sublanes:
                a = (e1 * (1.0 / z1) - e2 * (lam / z2)).astype(BF16)
                o = lax.dot_general(a, vh, (((0,), (0,)), ((), ())), preferred_element_type=F32)
            else:
                a = (e1 - e2 * (lam * z1 / z2)).astype(BF16)
                o = _dot(a, vh) * (1.0 / z1)
            o = (_rms(o) * subg_ref[hh:hh + 1, :]) * (1.0 - lam_init)
            y_s[qrows, sl] = o.astype(BF16)

    nqb = seq // qblk
    if nqb == 1:
        attend(0)
    else:
        def body(qb, carry):
            attend(pl.multiple_of(qb * qblk, qblk))
            return carry
        lax.fori_loop(0, nqb, body, 0)

    uv = _dot(h, win_ref[:, 3 * QKV_W + 3 * CONV_CH:IN_WIDTH])
    u, vc = uv[:, 0:CHUNK_CH], uv[:, CHUNK_CH:2 * CHUNK_CH]
    grp = lax.broadcasted_iota(jnp.int32, (CHUNK, CHUNK_CH), 1) // CHUNK_GDIM
    ws = [ws_ref[g].astype(BF16) for g in range(CHUNK_GROUPS)]
    y_chunk = []
    for n in range(rows // CHUNK):
        rs = slice(n * CHUNK, (n + 1) * CHUNK)
        vcn = vc[rs]
        sp = cb_ref[...]
        for g in range(CHUNK_GROUPS):
            sp = sp + _dot(ws[g], jnp.where(grp == g, vcn, 0.0).astype(BF16))
        y_chunk.append((u[rs] * sp).astype(BF16))

    split = QKV_W + CONV_CH
    y = _dot(y_s[:, 0:split], wout_ref[0:split, :])
    y = y + _dot(jnp.concatenate(y_chunk, axis=0), wout_ref[split:, :])
    o_ref[...] = x + gate * y


def _mixer_call(x, mods, norm_g, w_in, w_out, attn_lam, subln_g, conv_w, chunk_ws, chunk_bias,
                layer, mod_row0, rows_per_batch, seq, nseq, cache=None, prev_kv=None, side=()):
    rows_total = x.shape[0]
    rows = seq * nseq
    nsteps = rows_total // rows
    rope = cache is not None
    lam_init = 0.8 - 0.6 * math.exp(-0.3 * layer)
    resident = dict(pipeline_mode=pl.Buffered(1))
    const2 = lambda i: (0, 0)
    in_specs = [
        pl.BlockSpec((rows, D_MODEL), lambda i: (i, 0)),
        _mod_spec(layer, 1, mod_row0, rows, rows_per_batch),
        pl.BlockSpec((None, 3, D_MODEL), lambda i: (layer, 0, 0)),
        pl.BlockSpec((D_MODEL, IN_WIDTH), const2, **resident),
        pl.BlockSpec((D_MODEL, D_MODEL), const2, **resident),
        pl.BlockSpec((None, 4, ATTN_DH), lambda i: (layer, 0, 0)),
        pl.BlockSpec((None, ATTN_HEADS, HEAD_W), lambda i: (layer, 0, 0)),
        pl.BlockSpec((None, 3, CONV_CH), lambda i: (layer, 0, 0)),
        pl.BlockSpec((None, CHUNK_GROUPS, CHUNK, CHUNK), lambda i: (layer, 0, 0, 0)),
        pl.BlockSpec((None, CHUNK, CHUNK_CH), lambda i: (layer, 0, 0)),
    ]
    args = [x, mods, norm_g, w_in, w_out, attn_lam, subln_g, conv_w, chunk_ws, chunk_bias]
    x_out = jax.ShapeDtypeStruct((rows_total, D_MODEL), F32)
    x_spec = pl.BlockSpec((rows, D_MODEL), lambda i: (i, 0))
    if rope:
        cache_k, cache_v, cos_t, sin_t = cache
        past = cache_k.shape[3]
        kv_spec = pl.BlockSpec((None, None, ATTN_HEADS, past, HEAD_W), lambda i: (i, layer, 0, 0, 0))
        in_specs += [kv_spec, kv_spec,
                     pl.BlockSpec((seq, HEAD_W), const2), pl.BlockSpec((seq, HEAD_W), const2)]
        args += [cache_k, cache_v, cos_t, sin_t]
        out_shape, out_specs = [x_out], [x_spec]
        kv_rows = rows + past
    else:
        kv_rows = rows
        nb = rows_total // seq
        slab = lambda n: pl.BlockSpec((nseq, n, ATTN_HEADS, seq, HEAD_W), lambda i: (i, 0, 0, 0, 0))
        if layer > 0:
            in_specs += [slab(layer), slab(layer)]
            args += list(prev_kv)
        kv_out = jax.ShapeDtypeStruct((nb, layer + 1, ATTN_HEADS, seq, HEAD_W), F32)
        out_shape, out_specs = [x_out, kv_out, kv_out], [x_spec, slab(layer + 1), slab(layer + 1)]
    side_in, side_out, side_shapes = _side_cast_io(side, nsteps)
    args += [w for w, _ in side]
    return pl.pallas_call(
        functools.partial(_mixer_kernel, seq=seq, nseq=nseq, lam_init=lam_init, rope=rope, layer=layer,
                          n_side=len(side)),
        grid=(nsteps,),
        in_specs=in_specs + side_in,
        out_specs=out_specs + side_out,
        out_shape=out_shape + side_shapes,
        scratch_shapes=[pltpu.VMEM((rows, QKV_W), BF16)] * 2 + [pltpu.VMEM((kv_rows, QKV_W), BF16)] * 2
        + [pltpu.VMEM((rows, D_MODEL), BF16)],
        compiler_params=pltpu.CompilerParams(
            dimension_semantics=("arbitrary",), vmem_limit_bytes=V7X_VMEM_LIMIT_BYTES),
        name="mixer_latent" if rope else "mixer_context",
    )(*args)


def _rope_tables(n_tokens):
    n_rows = n_tokens // GRID_W
    row = np.repeat(np.arange(n_rows, dtype=np.float32), GRID_W)
    col = np.tile(np.arange(GRID_W, dtype=np.float32), n_rows)
    inv = np.float32(ROPE_BASE) ** (-np.arange(ROPE_FREQS, dtype=np.float32) / np.float32(ROPE_FREQS))
    ang = np.concatenate([row[:, None] * inv, col[:, None] * inv], axis=-1).astype(np.float32)
    cos, sin = np.cos(ang), np.sin(ang)
    return (jnp.asarray(np.tile(cos, (1, 4)), F32),
            jnp.asarray(np.tile(np.concatenate([-sin, sin], axis=-1), (1, 2)), F32))


def kernel(x_prompt, x_sample, cache_k, cache_v, c, c_ctx, w_mod, b_mod, norm_g, ffn1_w_gu, ffn1_w_d,
           ffn2_w_gu, ffn2_w_d, w_in, w_out, attn_lam, attn_subln_g, conv_w, chunk_ws, chunk_b,
           final_norm_g):
    nb_ctx, seq_ctx, _ = x_prompt.shape
    nb_lat, seq_lat, _ = x_sample.shape

    cond = jnp.concatenate(
        [c_ctx[None, :], c, jnp.zeros((COND_ROWS - 1 - nb_lat, D_MODEL), F32)], axis=0)
    mods = _mod_call(cond, w_mod, b_mod)

    chunk_bias = jnp.repeat(jnp.swapaxes(chunk_b, 1, 2), CHUNK_GDIM, axis=2)
    rope = _rope_tables(seq_lat)
    final_g = final_norm_g[None, :]

    xp = x_prompt.reshape(nb_ctx * seq_ctx, D_MODEL)
    xs = x_sample.reshape(nb_lat * seq_lat, D_MODEL)
    ctx_rows = dict(mod_row0=0, rows_per_batch=nb_ctx * seq_ctx)
    lat_rows = dict(mod_row0=1, rows_per_batch=seq_lat)
    new_kv = None
    w1_gu, w1_d = (ffn1_w_gu, 0), (ffn1_w_d, 0)
    for l in range(DEPTH):
        last = l == DEPTH - 1
        xp, = _ffn_call(xp, mods, norm_g, w1_gu, w1_d, l, 0, **ctx_rows)
        xs, w_in_l, w_out_l = _ffn_call(xs, mods, norm_g, w1_gu, w1_d, l, 0, **lat_rows,
                                        side=[(w_in, l), (w_out, l)])
        xp, *new_kv, w2_gu, w2_d = _mixer_call(
            xp, mods, norm_g, w_in_l, w_out_l, attn_lam, attn_subln_g, conv_w, chunk_ws, chunk_bias,
            l, seq=seq_ctx, nseq=CTX_SEQ_PER_STEP, prev_kv=new_kv,
            side=[(ffn2_w_gu, l), (ffn2_w_d, l)], **ctx_rows)
        xs, = _mixer_call(xs, mods, norm_g, w_in_l, w_out_l, attn_lam, attn_subln_g, conv_w, chunk_ws,
                          chunk_bias, l, seq=seq_lat, nseq=1, cache=(cache_k, cache_v) + rope, **lat_rows)
        fg = final_g if last else None
        xp, = _ffn_call(xp, mods, norm_g, w2_gu, w2_d, l, 2, final_g=fg, **ctx_rows)
        xs, *next_w1 = _ffn_call(xs, mods, norm_g, w2_gu, w2_d, l, 2, final_g=fg, **lat_rows,
                                 side=[] if last else [(ffn1_w_gu, l + 1), (ffn1_w_d, l + 1)])
        if not last:
            w1_gu, w1_d = next_w1

    return (xp.reshape(nb_ctx, seq_ctx, D_MODEL), xs.reshape(nb_lat, seq_lat, D_MODEL), *new_kv)
```

```python
import functools
import math

import jax
import jax.numpy as jnp
import numpy as np
from jax import lax
from jax.experimental import pallas as pl
from jax.experimental.pallas import tpu as pltpu

D_MODEL = 1024
D_FF = 2816
DEPTH = 2
N_MOD = 9
GRID_W = 64
ATTN_HEADS = 4
ATTN_DH = 64
HEAD_W = 2 * ATTN_DH
QKV_W = ATTN_HEADS * HEAD_W
CONV_CH = 256
CHUNK = 128
CHUNK_GROUPS = 4
CHUNK_CH = 256
CHUNK_GDIM = CHUNK_CH // CHUNK_GROUPS
IN_WIDTH = 3 * QKV_W + 3 * CONV_CH + 2 * CHUNK_CH
ROPE_BASE = 10000.0
ROPE_FREQS = ATTN_DH // 4
EPS = 1e-6
LOG2_E = math.log2(math.e)

COND_ROWS = 16
MOD_TILE_N = 2304
FFN_TILE_M = 512
FFN_CHUNKS = ((0, 768), (768, 768), (1536, 768), (2304, 512))
FFN_HEAD_SPLIT = 2
MIXER_HEAD_SPLIT = 2
Q_BLOCK = 512
SCORE_ELEMS_ISSUED_AHEAD = 128 * 1024
SCORE_UNITS_AHEAD = 1
CTX_SEQ_PER_STEP = 2
V7X_VMEM_LIMIT_BYTES = 56 * 1024 * 1024

BF16 = jnp.bfloat16
F32 = jnp.float32
BF16_SUBLANES = 16


def _dot(a, b):
    return jnp.dot(a, b, preferred_element_type=F32)


def _dot_nt(a, b):
    return lax.dot_general(a, b, (((1,), (1,)), ((), ())), preferred_element_type=F32)


def _rms(x):
    return x * lax.rsqrt(jnp.mean(x * x, axis=-1, keepdims=True) + EPS)


def _mod_norm(x, g, shift, scale):
    return _rms(x) * (g * (1.0 + scale)) + shift


def _mod_parts(mod_ref):
    m = mod_ref[...]
    return m[:, 0:D_MODEL], m[:, D_MODEL:2 * D_MODEL], m[:, 2 * D_MODEL:3 * D_MODEL]


def _mod_spec(layer, sub, row0, rows_per_step, rows_per_batch):
    return pl.BlockSpec((None, None, 1, 3 * D_MODEL),
                        lambda i: (layer, row0 + (i * rows_per_step) // rows_per_batch, 0, sub))


def _mod_kernel(c_ref, w_ref, b_ref, o_ref):
    a = jax.nn.silu(c_ref[...]).astype(BF16)
    m = _dot(a, w_ref[...].astype(BF16)) + b_ref[...]
    for r in range(COND_ROWS):
        o_ref[r] = m[r:r + 1, :]


def _mod_call(cond, w_mod, b_mod):
    n = N_MOD * D_MODEL
    return pl.pallas_call(
        _mod_kernel,
        grid=(DEPTH, n // MOD_TILE_N),
        in_specs=[
            pl.BlockSpec((COND_ROWS, D_MODEL), lambda l, j: (0, 0)),
            pl.BlockSpec((None, D_MODEL, MOD_TILE_N), lambda l, j: (l, 0, j)),
            pl.BlockSpec((None, 1, MOD_TILE_N), lambda l, j: (l, 0, j)),
        ],
        out_specs=pl.BlockSpec((None, COND_ROWS, 1, MOD_TILE_N), lambda l, j: (l, 0, 0, j)),
        out_shape=jax.ShapeDtypeStruct((DEPTH, COND_ROWS, 1, n), F32),
        compiler_params=pltpu.CompilerParams(dimension_semantics=("arbitrary", "arbitrary")),
        name="mod_rows",
    )(cond, w_mod, b_mod.reshape(DEPTH, 1, n))


def _cast_bands(src_refs, dst_refs):
    for src, dst in zip(src_refs, dst_refs):
        dst[...] = src[...].astype(BF16)


def _side_cast_io(side, nsteps):
    in_specs, out_specs, out_shapes = [], [], []
    for w, layer in side:
        _, r, c = w.shape
        band = r // nsteps
        assert band * nsteps == r and band % BF16_SUBLANES == 0, (w.shape, nsteps)
        in_specs.append(pl.BlockSpec((None, band, c), lambda i, layer=layer: (layer, i, 0)))
        out_specs.append(pl.BlockSpec((band, c), lambda i: (i, 0)))
        out_shapes.append(jax.ShapeDtypeStruct((r, c), BF16))
    return in_specs, out_specs, out_shapes


def _ffn_kernel(x_ref, mod_ref, g_ref, wgu_ref, wd_ref, *rest, norm_row, final_norm, n_side):
    if final_norm:
        fg_ref, rest = rest[0], rest[1:]
    side_in, o_ref, side_out = rest[:n_side], rest[n_side], rest[n_side + 1:]
    _cast_bands(side_in, side_out)
    x = x_ref[...]
    shift, scale, gate = _mod_parts(mod_ref)
    g_norm = g_ref[norm_row:norm_row + 1, :]
    rb = x.shape[0] // FFN_HEAD_SPLIT
    h_blocks = [_mod_norm(x[r * rb:(r + 1) * rb], g_norm, shift, scale).astype(BF16)
                for r in range(FFN_HEAD_SPLIT)]
    h = jnp.concatenate(h_blocks, axis=0)
    acc = None
    for c0, cw in FFN_CHUNKS:
        w_g = wgu_ref[:, c0:c0 + cw].astype(BF16)
        w_u = wgu_ref[:, D_FF + c0:D_FF + c0 + cw].astype(BF16)
        if c0 == 0:
            g = jnp.concatenate([_dot(hb, w_g) for hb in h_blocks], axis=0)
            u = jnp.concatenate([_dot(hb, w_u) for hb in h_blocks], axis=0)
        else:
            g, u = _dot(h, w_g), _dot(h, w_u)
        a = (jax.nn.silu(g) * u).astype(BF16)
        part = _dot(a, wd_ref[c0:c0 + cw, :].astype(BF16))
        acc = part if acc is None else acc + part
    y = x + 0.5 * gate * acc
    if final_norm:
        y = _rms(y) * fg_ref[...]
    o_ref[...] = y


def _ffn_call(x, mods, norm_g, w_gu, w_d, layer, sub, mod_row0, rows_per_batch, final_g=None, side=()):
    rows = x.shape[0]
    tm = FFN_TILE_M
    nsteps = rows // tm
    resident = dict(pipeline_mode=pl.Buffered(1))

    def weight_spec(w):
        if isinstance(w, tuple):
            stacked, w_layer = w
            return stacked, pl.BlockSpec((None,) + stacked.shape[1:], lambda i: (w_layer, 0, 0), **resident)
        return w, pl.BlockSpec(w.shape, lambda i: (0, 0), **resident)

    (w_gu, gu_spec), (w_d, d_spec) = weight_spec(w_gu), weight_spec(w_d)
    in_specs = [
        pl.BlockSpec((tm, D_MODEL), lambda i: (i, 0)),
        _mod_spec(layer, sub, mod_row0, tm, rows_per_batch),
        pl.BlockSpec((None, 3, D_MODEL), lambda i: (layer, 0, 0)),
        gu_spec,
        d_spec,
    ]
    args = [x, mods, norm_g, w_gu, w_d]
    if final_g is not None:
        in_specs.append(pl.BlockSpec((1, D_MODEL), lambda i: (0, 0)))
        args.append(final_g)
    side_in, side_out, side_shapes = _side_cast_io(side, nsteps)
    return pl.pallas_call(
        functools.partial(_ffn_kernel, norm_row=sub, final_norm=final_g is not None, n_side=len(side)),
        grid=(nsteps,),
        in_specs=in_specs + side_in,
        out_specs=[pl.BlockSpec((tm, D_MODEL), lambda i: (i, 0))] + side_out,
        out_shape=[jax.ShapeDtypeStruct((rows, D_MODEL), F32)] + side_shapes,
        compiler_params=pltpu.CompilerParams(
            dimension_semantics=("arbitrary",), vmem_limit_bytes=V7X_VMEM_LIMIT_BYTES),
        name="ffn",
    )(*args, *[w for w, _ in side])


def _exp2_sum(s, axis):
    e = jnp.exp2(s - jnp.max(s, axis=axis, keepdims=True))
    return e, jnp.sum(e, axis=axis, keepdims=True)


def _mixer_kernel(*refs, seq, nseq, lam_init, rope, layer, n_side):
    (x_ref, mod_ref, g_ref, win_ref, wout_ref, lam_ref, subg_ref,
     convw_ref, ws_ref, cb_ref) = refs[:10]
    refs = refs[10:]
    if rope:
        ck_ref, cv_ref, cos_ref, sin_ref = refs[:4]
        refs = refs[4:]
    elif layer > 0:
        pk_ref, pv_ref = refs[:2]
        refs = refs[2:]
    side_in, o_ref, refs = refs[:n_side], refs[n_side], refs[n_side + 1:]
    if not rope:
        nk_ref, nv_ref = refs[:2]
        refs = refs[2:]
        if layer > 0:
            nk_ref[:, 0:layer] = pk_ref[...]
            nv_ref[:, 0:layer] = pv_ref[...]
        nk_ref, nv_ref = nk_ref.at[:, layer], nv_ref.at[:, layer]
    side_out, refs = refs[:n_side], refs[n_side:]
    _cast_bands(side_in, side_out)
    q1_s, q2_s, k_s, v_s, y_s = refs
    rows = seq * nseq

    x = x_ref[...]
    shift, scale, gate = _mod_parts(mod_ref)
    g_norm = g_ref[1:2, :]
    rb = rows // MIXER_HEAD_SPLIT
    h_blocks = [_mod_norm(x[r * rb:(r + 1) * rb], g_norm, shift, scale).astype(BF16)
                for r in range(MIXER_HEAD_SPLIT)]
    h = jnp.concatenate(h_blocks, axis=0)

    lp = lam_ref[...]
    lam = (jnp.exp(jnp.sum(lp[0:1] * lp[1:2], axis=-1, keepdims=True))
           - jnp.exp(jnp.sum(lp[2:3] * lp[3:4], axis=-1, keepdims=True)) + lam_init)

    lane = lax.broadcasted_iota(jnp.int32, (rows, HEAD_W), 1)
    first_map = lane < ATTN_DH
    if rope:
        first_half = (lane & (ATTN_DH - 1)) < (ATTN_DH // 2)
        cos_t, sin_t = cos_ref[...], sin_ref[...]

        def rotate(t):
            partner = jnp.where(first_half, pltpu.roll(t, HEAD_W - ATTN_DH // 2, 1),
                                pltpu.roll(t, ATTN_DH // 2, 1))
            return t * cos_t + partner * sin_t

    qf = jnp.concatenate([_dot(hb, win_ref[:, 0:QKV_W]) for hb in h_blocks], axis=0)
    kf = _dot(h, win_ref[:, QKV_W:2 * QKV_W])
    vf = _dot(h, win_ref[:, 2 * QKV_W:3 * QKV_W])
    v_s[0:rows, :] = vf.astype(BF16)
    for hh in range(ATTN_HEADS):
        sl = slice(hh * HEAD_W, (hh + 1) * HEAD_W)
        qh, kh = qf[:, sl], kf[:, sl]
        if rope:
            qh, kh = rotate(qh), rotate(kh)
            k_s[rows:, sl] = ck_ref[hh].astype(BF16)
            v_s[rows:, sl] = cv_ref[hh].astype(BF16)
        else:
            for s in range(nseq):
                nk_ref[s, hh] = kh[s * seq:(s + 1) * seq]
                nv_ref[s, hh] = vf[s * seq:(s + 1) * seq, sl]
        qh = qh * (ATTN_DH ** -0.5 * LOG2_E)
        q1_s[:, sl] = jnp.where(first_map, qh, 0.0).astype(BF16)
        q2_s[:, sl] = jnp.where(first_map, 0.0, qh).astype(BF16)
        k_s[0:rows, sl] = kh.astype(BF16)

    conv0 = 3 * QKV_W

    def proj(c0):
        return _dot(h, win_ref[:, c0:c0 + CONV_CH])

    def conv(gb, gc, hc):
        xc = gc * hc
        pos = lax.broadcasted_iota(jnp.int32, (rows, CONV_CH), 0) & (seq - 1)
        prev = jnp.where(pos == 0, 0.0, pltpu.roll(xc, 1, 0))
        nxt = jnp.where(pos == seq - 1, 0.0, pltpu.roll(xc, rows - 1, 0))
        cw = convw_ref[...]
        y_s[:, QKV_W:QKV_W + CONV_CH] = (gb * (cw[0:1] * prev + cw[1:2] * xc + cw[2:3] * nxt)).astype(BF16)

    grp = lax.broadcasted_iota(jnp.int32, (CHUNK, CHUNK_CH), 1) // CHUNK_GDIM
    ws = [ws_ref[g].astype(BF16) for g in range(CHUNK_GROUPS)]

    def chunk_mlp(n, u, vc):
        rs = slice(n * CHUNK, (n + 1) * CHUNK)
        vcn = vc[rs]
        sp = cb_ref[...]
        for g in range(CHUNK_GROUPS):
            sp = sp + _dot(ws[g], jnp.where(grp == g, vcn, 0.0).astype(BF16))
        return (u[rs] * sp).astype(BF16)

    kv_len = k_s.shape[0] // nseq
    qblk = min(Q_BLOCK, seq)
    keys_on_sublanes = qblk * kv_len <= SCORE_ELEMS_ISSUED_AHEAD
    key_axis = 0 if keys_on_sublanes else 1
    units = [(s, qb, hh) for s in range(nseq) for qb in range(seq // qblk) for hh in range(ATTN_HEADS)]

    def scores(unit):
        s, qb, hh = unit
        sl = slice(hh * HEAD_W, (hh + 1) * HEAD_W)
        kh = k_s[s * kv_len:(s + 1) * kv_len, sl]
        qrows = slice(s * seq + qb * qblk, s * seq + (qb + 1) * qblk)
        if keys_on_sublanes:
            return _dot_nt(kh, q1_s[qrows, sl]), _dot_nt(kh, q2_s[qrows, sl])
        return _dot_nt(q1_s[qrows, sl], kh), _dot_nt(q2_s[qrows, sl], kh)

    def attend(unit, s1, s2):
        s, qb, hh = unit
        sl = slice(hh * HEAD_W, (hh + 1) * HEAD_W)
        vh = v_s[s * kv_len:(s + 1) * kv_len, sl]
        qrows = slice(s * seq + qb * qblk, s * seq + (qb + 1) * qblk)
        e1, z1 = _exp2_sum(s1, key_axis)
        e2, z2 = _exp2_sum(s2, key_axis)
        if keys_on_sublanes:
            a = (e1 * (1.0 / z1) - e2 * (lam / z2)).astype(BF16)
            o = lax.dot_general(a, vh, (((0,), (0,)), ((), ())), preferred_element_type=F32)
        else:
            a = (e1 - e2 * (lam * z1 / z2)).astype(BF16)
            o = _dot(a, vh) * (1.0 / z1)
        o = (_rms(o) * subg_ref[hh:hh + 1, :]) * (1.0 - lam_init)
        y_s[qrows, sl] = o.astype(BF16)

    split = QKV_W + CONV_CH
    if keys_on_sublanes:
        conv(proj(conv0), proj(conv0 + CONV_CH), proj(conv0 + 2 * CONV_CH))
        all_scores = [scores(unit) for unit in units]
        for unit, (s1, s2) in zip(units, all_scores):
            attend(unit, s1, s2)
        u, vc = proj(conv0 + 3 * CONV_CH), proj(conv0 + 4 * CONV_CH)
        y_chunk = jnp.concatenate([chunk_mlp(n, u, vc) for n in range(rows // CHUNK)], axis=0)
        y = _dot(y_s[:, 0:split], wout_ref[0:split, :]) + _dot(y_chunk, wout_ref[split:, :])
        o_ref[...] = x + gate * y
    else:
        n_blocks, per_block = seq // qblk, ATTN_HEADS
        assert nseq == 1 and len(units) > 5, (nseq, len(units))
        vals = {}
        col_w = D_MODEL // per_block

        def out_piece(b, n):
            rws, cols = slice(b * qblk, (b + 1) * qblk), slice(n * col_w, (n + 1) * col_w)
            o_ref[rws, cols] = x[rws, cols] + gate[:, cols] * _dot(y_s[rws, :], wout_ref[:, cols])

        def chunks_of(b):
            for n in range(b * qblk // CHUNK, (b + 1) * qblk // CHUNK):
                y_s[n * CHUNK:(n + 1) * CHUNK, split:] = chunk_mlp(n, vals["u"], vals["vc"])

        fill = {i: [] for i in range(len(units))}
        fill[0].append(lambda: vals.update(gb=proj(conv0)))
        fill[1].append(lambda: vals.update(gc=proj(conv0 + CONV_CH)))
        fill[2].append(lambda: conv(vals["gb"], vals["gc"], proj(conv0 + 2 * CONV_CH)))
        fill[3].append(lambda: vals.update(u=proj(conv0 + 3 * CONV_CH)))
        fill[4].append(lambda: vals.update(vc=proj(conv0 + 4 * CONV_CH)))
        tail = []
        for b in range(n_blocks):
            slot = min(5 + b, len(units) - 1)
            fill[slot].append(functools.partial(chunks_of, b))
            start = max(slot, per_block * (b + 1))
            for n in range(per_block):
                (fill[start + n] if start + n < len(units) else tail).append(functools.partial(out_piece, b, n))
        nxt = scores(units[0])
        for i, unit in enumerate(units):
            s1, s2 = nxt
            if i + 1 < len(units):
                nxt = scores(units[i + 1])
            for piece in fill[i]:
                piece()
            attend(unit, s1, s2)
        for piece in tail:
            piece()


def _mixer_call(x, mods, norm_g, w_in, w_out, attn_lam, subln_g, conv_w, chunk_ws, chunk_bias,
                layer, mod_row0, rows_per_batch, seq, nseq, cache=None, prev_kv=None, side=()):
    rows_total = x.shape[0]
    rows = seq * nseq
    nsteps = rows_total // rows
    rope = cache is not None
    lam_init = 0.8 - 0.6 * math.exp(-0.3 * layer)
    resident = dict(pipeline_mode=pl.Buffered(1))
    const2 = lambda i: (0, 0)
    in_specs = [
        pl.BlockSpec((rows, D_MODEL), lambda i: (i, 0)),
        _mod_spec(layer, 1, mod_row0, rows, rows_per_batch),
        pl.BlockSpec((None, 3, D_MODEL), lambda i: (layer, 0, 0)),
        pl.BlockSpec((D_MODEL, IN_WIDTH), const2, **resident),
        pl.BlockSpec((D_MODEL, D_MODEL), const2, **resident),
        pl.BlockSpec((None, 4, ATTN_DH), lambda i: (layer, 0, 0)),
        pl.BlockSpec((None, ATTN_HEADS, HEAD_W), lambda i: (layer, 0, 0)),
        pl.BlockSpec((None, 3, CONV_CH), lambda i: (layer, 0, 0)),
        pl.BlockSpec((None, CHUNK_GROUPS, CHUNK, CHUNK), lambda i: (layer, 0, 0, 0)),
        pl.BlockSpec((None, CHUNK, CHUNK_CH), lambda i: (layer, 0, 0)),
    ]
    args = [x, mods, norm_g, w_in, w_out, attn_lam, subln_g, conv_w, chunk_ws, chunk_bias]
    x_out = jax.ShapeDtypeStruct((rows_total, D_MODEL), F32)
    x_spec = pl.BlockSpec((rows, D_MODEL), lambda i: (i, 0))
    if rope:
        cache_k, cache_v, cos_t, sin_t = cache
        past = cache_k.shape[3]
        kv_spec = pl.BlockSpec((None, None, ATTN_HEADS, past, HEAD_W), lambda i: (i, layer, 0, 0, 0))
        in_specs += [kv_spec, kv_spec,
                     pl.BlockSpec((seq, HEAD_W), const2), pl.BlockSpec((seq, HEAD_W), const2)]
        args += [cache_k, cache_v, cos_t, sin_t]
        out_shape, out_specs = [x_out], [x_spec]
        kv_rows = rows + past
    else:
        kv_rows = rows
        nb = rows_total // seq
        slab = lambda n: pl.BlockSpec((nseq, n, ATTN_HEADS, seq, HEAD_W), lambda i: (i, 0, 0, 0, 0))
        if layer > 0:
            in_specs += [slab(layer), slab(layer)]
            args += list(prev_kv)
        kv_out = jax.ShapeDtypeStruct((nb, layer + 1, ATTN_HEADS, seq, HEAD_W), F32)
        out_shape, out_specs = [x_out, kv_out, kv_out], [x_spec, slab(layer + 1), slab(layer + 1)]
    side_in, side_out, side_shapes = _side_cast_io(side, nsteps)
    args += [w for w, _ in side]
    return pl.pallas_call(
        functools.partial(_mixer_kernel, seq=seq, nseq=nseq, lam_init=lam_init, rope=rope, layer=layer,
                          n_side=len(side)),
        grid=(nsteps,),
        in_specs=in_specs + side_in,
        out_specs=out_specs + side_out,
        out_shape=out_shape + side_shapes,
        scratch_shapes=[pltpu.VMEM((rows, QKV_W), BF16)] * 2 + [pltpu.VMEM((kv_rows, QKV_W), BF16)] * 2
        + [pltpu.VMEM((rows, D_MODEL), BF16)],
        compiler_params=pltpu.CompilerParams(
            dimension_semantics=("arbitrary",), vmem_limit_bytes=V7X_VMEM_LIMIT_BYTES),
        name="mixer_latent" if rope else "mixer_context",
    )(*args)


def _rope_tables(n_tokens):
    n_rows = n_tokens // GRID_W
    row = np.repeat(np.arange(n_rows, dtype=np.float32), GRID_W)
    col = np.tile(np.arange(GRID_W, dtype=np.float32), n_rows)
    inv = np.float32(ROPE_BASE) ** (-np.arange(ROPE_FREQS, dtype=np.float32) / np.float32(ROPE_FREQS))
    ang = np.concatenate([row[:, None] * inv, col[:, None] * inv], axis=-1).astype(np.float32)
    cos, sin = np.cos(ang), np.sin(ang)
    return (jnp.asarray(np.tile(cos, (1, 4)), F32),
            jnp.asarray(np.tile(np.concatenate([-sin, sin], axis=-1), (1, 2)), F32))


def kernel(x_prompt, x_sample, cache_k, cache_v, c, c_ctx, w_mod, b_mod, norm_g, ffn1_w_gu, ffn1_w_d,
           ffn2_w_gu, ffn2_w_d, w_in, w_out, attn_lam, attn_subln_g, conv_w, chunk_ws, chunk_b,
           final_norm_g):
    nb_ctx, seq_ctx, _ = x_prompt.shape
    nb_lat, seq_lat, _ = x_sample.shape

    cond = jnp.concatenate(
        [c_ctx[None, :], c, jnp.zeros((COND_ROWS - 1 - nb_lat, D_MODEL), F32)], axis=0)
    mods = _mod_call(cond, w_mod, b_mod)

    chunk_bias = jnp.repeat(jnp.swapaxes(chunk_b, 1, 2), CHUNK_GDIM, axis=2)
    rope = _rope_tables(seq_lat)
    final_g = final_norm_g[None, :]

    xp = x_prompt.reshape(nb_ctx * seq_ctx, D_MODEL)
    xs = x_sample.reshape(nb_lat * seq_lat, D_MODEL)
    ctx_rows = dict(mod_row0=0, rows_per_batch=nb_ctx * seq_ctx)
    lat_rows = dict(mod_row0=1, rows_per_batch=seq_lat)
    new_kv = None
    w1_gu, w1_d = (ffn1_w_gu, 0), (ffn1_w_d, 0)
    for l in range(DEPTH):
        last = l == DEPTH - 1
        xp, = _ffn_call(xp, mods, norm_g, w1_gu, w1_d, l, 0, **ctx_rows)
        xs, w_in_l, w_out_l = _ffn_call(xs, mods, norm_g, w1_gu, w1_d, l, 0, **lat_rows,
                                        side=[(w_in, l), (w_out, l)])
        xp, *new_kv, w2_gu, w2_d = _mixer_call(
            xp, mods, norm_g, w_in_l, w_out_l, attn_lam, attn_subln_g, conv_w, chunk_ws, chunk_bias,
            l, seq=seq_ctx, nseq=CTX_SEQ_PER_STEP, prev_kv=new_kv,
            side=[(ffn2_w_gu, l), (ffn2_w_d, l)], **ctx_rows)
        xs, = _mixer_call(xs, mods, norm_g, w_in_l, w_out_l, attn_lam, attn_subln_g, conv_w, chunk_ws,
                          chunk_bias, l, seq=seq_lat, nseq=1, cache=(cache_k, cache_v) + rope, **lat_rows)
        fg = final_g if last else None
        xp, = _ffn_call(xp, mods, norm_g, w2_gu, w2_d, l, 2, final_g=fg, **ctx_rows)
        xs, *next_w1 = _ffn_call(xs, mods, norm_g, w2_gu, w2_d, l, 2, final_g=fg, **lat_rows,
                                 side=[] if last else [(ffn1_w_gu, l + 1), (ffn1_w_d, l + 1)])
        if not last:
            w1_gu, w1_d = next_w1

    return (xp.reshape(nb_ctx, seq_ctx, D_MODEL), xs.reshape(nb_lat, seq_lat, D_MODEL), *new_kv)
```

```python
import functools
import math

import jax
import jax.numpy as jnp
import numpy as np
from jax import lax
from jax.experimental import pallas as pl
from jax.experimental.pallas import tpu as pltpu

D_MODEL = 1024
D_FF = 2816
DEPTH = 2
N_MOD = 9
GRID_W = 64
ATTN_HEADS = 4
ATTN_DH = 64
HEAD_W = 2 * ATTN_DH
QKV_W = ATTN_HEADS * HEAD_W
CONV_CH = 256
CHUNK = 128
CHUNK_GROUPS = 4
CHUNK_CH = 256
CHUNK_GDIM = CHUNK_CH // CHUNK_GROUPS
IN_WIDTH = 3 * QKV_W + 3 * CONV_CH + 2 * CHUNK_CH
ROPE_BASE = 10000.0
ROPE_FREQS = ATTN_DH // 4
EPS = 1e-6
LOG2_E = math.log2(math.e)

COND_ROWS = 16
MOD_TILE_N = 2304
FFN_TILE_M = 512
FFN_CHUNKS = ((0, 768), (768, 768), (1536, 768), (2304, 512))
FFN_HEAD_SPLIT = 2
MIXER_HEAD_SPLIT = 2
Q_BLOCK = 512
SCORE_ELEMS_ISSUED_AHEAD = 128 * 1024
SCORE_UNITS_AHEAD = 1
CTX_SEQ_PER_STEP = 2
V7X_VMEM_LIMIT_BYTES = 56 * 1024 * 1024

BF16 = jnp.bfloat16
F32 = jnp.float32
BF16_SUBLANES = 16


def _dot(a, b):
    return jnp.dot(a, b, preferred_element_type=F32)


def _dot_nt(a, b):
    return lax.dot_general(a, b, (((1,), (1,)), ((), ())), preferred_element_type=F32)


def _rms(x):
    return x * lax.rsqrt(jnp.mean(x * x, axis=-1, keepdims=True) + EPS)


def _mod_norm(x, g, shift, scale):
    return _rms(x) * (g * (1.0 + scale)) + shift


def _mod_parts(mod_ref):
    m = mod_ref[...]
    return m[:, 0:D_MODEL], m[:, D_MODEL:2 * D_MODEL], m[:, 2 * D_MODEL:3 * D_MODEL]


def _mod_spec(layer, sub, row0, rows_per_step, rows_per_batch):
    return pl.BlockSpec((None, None, 1, 3 * D_MODEL),
                        lambda i: (layer, row0 + (i * rows_per_step) // rows_per_batch, 0, sub))


def _mod_kernel(c_ref, w_ref, b_ref, o_ref):
    a = jax.nn.silu(c_ref[...]).astype(BF16)
    m = _dot(a, w_ref[...].astype(BF16)) + b_ref[...]
    for r in range(COND_ROWS):
        o_ref[r] = m[r:r + 1, :]


def _mod_call(cond, w_mod, b_mod):
    n = N_MOD * D_MODEL
    return pl.pallas_call(
        _mod_kernel,
        grid=(DEPTH, n // MOD_TILE_N),
        in_specs=[
            pl.BlockSpec((COND_ROWS, D_MODEL), lambda l, j: (0, 0)),
            pl.BlockSpec((None, D_MODEL, MOD_TILE_N), lambda l, j: (l, 0, j)),
            pl.BlockSpec((None, 1, MOD_TILE_N), lambda l, j: (l, 0, j)),
        ],
        out_specs=pl.BlockSpec((None, COND_ROWS, 1, MOD_TILE_N), lambda l, j: (l, 0, 0, j)),
        out_shape=jax.ShapeDtypeStruct((DEPTH, COND_ROWS, 1, n), F32),
        compiler_params=pltpu.CompilerParams(dimension_semantics=("arbitrary", "arbitrary")),
        name="mod_rows",
    )(cond, w_mod, b_mod.reshape(DEPTH, 1, n))


def _cast_bands(src_refs, dst_refs):
    for src, dst in zip(src_refs, dst_refs):
        dst[...] = src[...].astype(BF16)


def _side_cast_io(side, nsteps):
    in_specs, out_specs, out_shapes = [], [], []
    for w, layer in side:
        _, r, c = w.shape
        band = r // nsteps
        assert band * nsteps == r and band % BF16_SUBLANES == 0, (w.shape, nsteps)
        in_specs.append(pl.BlockSpec((None, band, c), lambda i, layer=layer: (layer, i, 0)))
        out_specs.append(pl.BlockSpec((band, c), lambda i: (i, 0)))
        out_shapes.append(jax.ShapeDtypeStruct((r, c), BF16))
    return in_specs, out_specs, out_shapes


def _ffn_kernel(x_ref, g_ref, *rest, norm_rows, final_norm, n_side):
    n_sub = len(norm_rows)
    subs, rest = [rest[3 * j:3 * j + 3] for j in range(n_sub)], rest[3 * n_sub:]
    if final_norm:
        fg_ref, rest = rest[0], rest[1:]
    side_in, o_ref, side_out = rest[:n_side], rest[n_side], rest[n_side + 1:]
    _cast_bands(side_in, side_out)
    y = x_ref[...]
    for (layer, row), (mod_ref, wgu_ref, wd_ref) in zip(norm_rows, subs):
        y = _ffn_sublayer(y, _mod_parts(mod_ref), g_ref[layer, row:row + 1, :], wgu_ref, wd_ref)
    if final_norm:
        y = _rms(y) * fg_ref[...]
    o_ref[...] = y


def _ffn_sublayer(x, mod, g_norm, wgu_ref, wd_ref):
    shift, scale, gate = mod
    rb = x.shape[0] // FFN_HEAD_SPLIT
    h_blocks = [_mod_norm(x[r * rb:(r + 1) * rb], g_norm, shift, scale).astype(BF16)
                for r in range(FFN_HEAD_SPLIT)]
    h = jnp.concatenate(h_blocks, axis=0)
    acc = None
    for c0, cw in FFN_CHUNKS:
        w_g = wgu_ref[:, c0:c0 + cw].astype(BF16)
        w_u = wgu_ref[:, D_FF + c0:D_FF + c0 + cw].astype(BF16)
        if c0 == 0:
            g = jnp.concatenate([_dot(hb, w_g) for hb in h_blocks], axis=0)
            u = jnp.concatenate([_dot(hb, w_u) for hb in h_blocks], axis=0)
        else:
            g, u = _dot(h, w_g), _dot(h, w_u)
        a = (jax.nn.silu(g) * u).astype(BF16)
        part = _dot(a, wd_ref[c0:c0 + cw, :].astype(BF16))
        acc = part if acc is None else acc + part
    return x + 0.5 * gate * acc


def _ffn_call(x, mods, norm_g, sublayers, mod_row0, rows_per_batch, final_g=None, side=()):
    rows = x.shape[0]
    tm = FFN_TILE_M
    nsteps = rows // tm
    resident = dict(pipeline_mode=pl.Buffered(1))

    def weight_spec(w):
        if isinstance(w, tuple):
            stacked, w_layer = w
            return stacked, pl.BlockSpec((None,) + stacked.shape[1:], lambda i: (w_layer, 0, 0), **resident)
        return w, pl.BlockSpec(w.shape, lambda i: (0, 0), **resident)

    in_specs = [pl.BlockSpec((tm, D_MODEL), lambda i: (i, 0)),
                pl.BlockSpec(norm_g.shape, lambda i: (0, 0, 0))]
    args = [x, norm_g]
    for layer, sub, w_gu, w_d in sublayers:
        (w_gu, gu_spec), (w_d, d_spec) = weight_spec(w_gu), weight_spec(w_d)
        in_specs += [_mod_spec(layer, sub, mod_row0, tm, rows_per_batch), gu_spec, d_spec]
        args += [mods, w_gu, w_d]
    if final_g is not None:
        in_specs.append(pl.BlockSpec((1, D_MODEL), lambda i: (0, 0)))
        args.append(final_g)
    side_in, side_out, side_shapes = _side_cast_io(side, nsteps)
    return pl.pallas_call(
        functools.partial(_ffn_kernel, norm_rows=tuple((layer, sub) for layer, sub, _, _ in sublayers),
                          final_norm=final_g is not None, n_side=len(side)),
        grid=(nsteps,),
        in_specs=in_specs + side_in,
        out_specs=[pl.BlockSpec((tm, D_MODEL), lambda i: (i, 0))] + side_out,
        out_shape=[jax.ShapeDtypeStruct((rows, D_MODEL), F32)] + side_shapes,
        compiler_params=pltpu.CompilerParams(
            dimension_semantics=("arbitrary",), vmem_limit_bytes=V7X_VMEM_LIMIT_BYTES),
        name="ffn",
    )(*args, *[w for w, _ in side])


def _exp2_sum(s, axis):
    e = jnp.exp2(s - jnp.max(s, axis=axis, keepdims=True))
    return e, jnp.sum(e, axis=axis, keepdims=True)


def _mixer_kernel(*refs, seq, nseq, lam_init, rope, layer, n_side):
    (x_ref, mod_ref, g_ref, win_ref, wout_ref, lam_ref, subg_ref,
     convw_ref, ws_ref, cb_ref) = refs[:10]
    refs = refs[10:]
    if rope:
        ck_ref, cv_ref, cos_ref, sin_ref = refs[:4]
        refs = refs[4:]
    elif layer > 0:
        pk_ref, pv_ref = refs[:2]
        refs = refs[2:]
    side_in, o_ref, refs = refs[:n_side], refs[n_side], refs[n_side + 1:]
    if not rope:
        nk_ref, nv_ref = refs[:2]
        refs = refs[2:]
        if layer > 0:
            nk_ref[:, 0:layer] = pk_ref[...]
            nv_ref[:, 0:layer] = pv_ref[...]
        nk_ref, nv_ref = nk_ref.at[:, layer], nv_ref.at[:, layer]
    side_out, refs = refs[:n_side], refs[n_side:]
    _cast_bands(side_in, side_out)
    q1_s, q2_s, k_s, v_s, y_s = refs
    rows = seq * nseq

    x = x_ref[...]
    shift, scale, gate = _mod_parts(mod_ref)
    g_norm = g_ref[1:2, :]
    rb = rows // MIXER_HEAD_SPLIT
    h_blocks = [_mod_norm(x[r * rb:(r + 1) * rb], g_norm, shift, scale).astype(BF16)
                for r in range(MIXER_HEAD_SPLIT)]
    h = jnp.concatenate(h_blocks, axis=0)

    lp = lam_ref[...]
    lam = (jnp.exp(jnp.sum(lp[0:1] * lp[1:2], axis=-1, keepdims=True))
           - jnp.exp(jnp.sum(lp[2:3] * lp[3:4], axis=-1, keepdims=True)) + lam_init)

    lane = lax.broadcasted_iota(jnp.int32, (rows, HEAD_W), 1)
    first_map = lane < ATTN_DH
    if rope:
        first_half = (lane & (ATTN_DH - 1)) < (ATTN_DH // 2)
        cos_t, sin_t = cos_ref[...], sin_ref[...]

        def rotate(t):
            partner = jnp.where(first_half, pltpu.roll(t, HEAD_W - ATTN_DH // 2, 1),
                                pltpu.roll(t, ATTN_DH // 2, 1))
            return t * cos_t + partner * sin_t

    qf = jnp.concatenate([_dot(hb, win_ref[:, 0:QKV_W]) for hb in h_blocks], axis=0)
    kf = _dot(h, win_ref[:, QKV_W:2 * QKV_W])
    vf = _dot(h, win_ref[:, 2 * QKV_W:3 * QKV_W])
    v_s[0:rows, :] = vf.astype(BF16)
    for hh in range(ATTN_HEADS):
        sl = slice(hh * HEAD_W, (hh + 1) * HEAD_W)
        qh, kh = qf[:, sl], kf[:, sl]
        if rope:
            qh, kh = rotate(qh), rotate(kh)
            k_s[rows:, sl] = ck_ref[hh].astype(BF16)
            v_s[rows:, sl] = cv_ref[hh].astype(BF16)
        else:
            for s in range(nseq):
                nk_ref[s, hh] = kh[s * seq:(s + 1) * seq]
                nv_ref[s, hh] = vf[s * seq:(s + 1) * seq, sl]
        qh = qh * (ATTN_DH ** -0.5 * LOG2_E)
        q1_s[:, sl] = jnp.where(first_map, qh, 0.0).astype(BF16)
        q2_s[:, sl] = jnp.where(first_map, 0.0, qh).astype(BF16)
        k_s[0:rows, sl] = kh.astype(BF16)

    conv0 = 3 * QKV_W

    def proj(c0):
        return _dot(h, win_ref[:, c0:c0 + CONV_CH])

    def conv(gb, gc, hc):
        xc = gc * hc
        pos = lax.broadcasted_iota(jnp.int32, (rows, CONV_CH), 0) & (seq - 1)
        prev = jnp.where(pos == 0, 0.0, pltpu.roll(xc, 1, 0))
        nxt = jnp.where(pos == seq - 1, 0.0, pltpu.roll(xc, rows - 1, 0))
        cw = convw_ref[...]
        y_s[:, QKV_W:QKV_W + CONV_CH] = (gb * (cw[0:1] * prev + cw[1:2] * xc + cw[2:3] * nxt)).astype(BF16)

    grp = lax.broadcasted_iota(jnp.int32, (CHUNK, CHUNK_CH), 1) // CHUNK_GDIM
    ws = [ws_ref[g].astype(BF16) for g in range(CHUNK_GROUPS)]

    def chunk_mlp(n, u, vc):
        rs = slice(n * CHUNK, (n + 1) * CHUNK)
        vcn = vc[rs]
        sp = cb_ref[...]
        for g in range(CHUNK_GROUPS):
            sp = sp + _dot(ws[g], jnp.where(grp == g, vcn, 0.0).astype(BF16))
        return (u[rs] * sp).astype(BF16)

    kv_len = k_s.shape[0] // nseq
    qblk = min(Q_BLOCK, seq)
    keys_on_sublanes = qblk * kv_len <= SCORE_ELEMS_ISSUED_AHEAD
    key_axis = 0 if keys_on_sublanes else 1
    units = [(s, qb, hh) for s in range(nseq) for qb in range(seq // qblk) for hh in range(ATTN_HEADS)]

    def scores(unit):
        s, qb, hh = unit
        sl = slice(hh * HEAD_W, (hh + 1) * HEAD_W)
        kh = k_s[s * kv_len:(s + 1) * kv_len, sl]
        qrows = slice(s * seq + qb * qblk, s * seq + (qb + 1) * qblk)
        if keys_on_sublanes:
            return _dot_nt(kh, q1_s[qrows, sl]), _dot_nt(kh, q2_s[qrows, sl])
        return _dot_nt(q1_s[qrows, sl], kh), _dot_nt(q2_s[qrows, sl], kh)

    def attend(unit, s1, s2):
        s, qb, hh = unit
        sl = slice(hh * HEAD_W, (hh + 1) * HEAD_W)
        vh = v_s[s * kv_len:(s + 1) * kv_len, sl]
        qrows = slice(s * seq + qb * qblk, s * seq + (qb + 1) * qblk)
        e1, z1 = _exp2_sum(s1, key_axis)
        e2, z2 = _exp2_sum(s2, key_axis)
        if keys_on_sublanes:
            a = (e1 * (1.0 / z1) - e2 * (lam / z2)).astype(BF16)
            o = lax.dot_general(a, vh, (((0,), (0,)), ((), ())), preferred_element_type=F32)
        else:
            a = (e1 - e2 * (lam * z1 / z2)).astype(BF16)
            o = _dot(a, vh) * (1.0 / z1)
        o = (_rms(o) * subg_ref[hh:hh + 1, :]) * (1.0 - lam_init)
        y_s[qrows, sl] = o.astype(BF16)

    split = QKV_W + CONV_CH
    if keys_on_sublanes:
        conv(proj(conv0), proj(conv0 + CONV_CH), proj(conv0 + 2 * CONV_CH))
        all_scores = [scores(unit) for unit in units]
        for unit, (s1, s2) in zip(units, all_scores):
            attend(unit, s1, s2)
        u, vc = proj(conv0 + 3 * CONV_CH), proj(conv0 + 4 * CONV_CH)
        y_chunk = jnp.concatenate([chunk_mlp(n, u, vc) for n in range(rows // CHUNK)], axis=0)
        y = _dot(y_s[:, 0:split], wout_ref[0:split, :]) + _dot(y_chunk, wout_ref[split:, :])
        o_ref[...] = x + gate * y
    else:
        n_blocks, per_block = nseq * (seq // qblk), ATTN_HEADS
        assert len(units) > 5, len(units)
        vals = {}
        col_w = D_MODEL // per_block

        def out_piece(b, n):
            rws, cols = slice(b * qblk, (b + 1) * qblk), slice(n * col_w, (n + 1) * col_w)
            o_ref[rws, cols] = x[rws, cols] + gate[:, cols] * _dot(y_s[rws, :], wout_ref[:, cols])

        def chunks_of(b):
            for n in range(b * qblk // CHUNK, (b + 1) * qblk // CHUNK):
                y_s[n * CHUNK:(n + 1) * CHUNK, split:] = chunk_mlp(n, vals["u"], vals["vc"])

        fill = {i: [] for i in range(len(units))}
        fill[0].append(lambda: vals.update(gb=proj(conv0)))
        fill[1].append(lambda: vals.update(gc=proj(conv0 + CONV_CH)))
        fill[2].append(lambda: conv(vals["gb"], vals["gc"], proj(conv0 + 2 * CONV_CH)))
        fill[3].append(lambda: vals.update(u=proj(conv0 + 3 * CONV_CH)))
        fill[4].append(lambda: vals.update(vc=proj(conv0 + 4 * CONV_CH)))
        tail = []
        for b in range(n_blocks):
            slot = min(5 + b, len(units) - 1)
            fill[slot].append(functools.partial(chunks_of, b))
            start = max(slot, per_block * (b + 1))
            for n in range(per_block):
                (fill[start + n] if start + n < len(units) else tail).append(functools.partial(out_piece, b, n))
        queue = [scores(unit) for unit in units[:SCORE_UNITS_AHEAD]]
        for i, unit in enumerate(units):
            s1, s2 = queue.pop(0)
            if i + SCORE_UNITS_AHEAD < len(units):
                queue.append(scores(units[i + SCORE_UNITS_AHEAD]))
            for piece in fill[i]:
                piece()
            attend(unit, s1, s2)
        for piece in tail:
            piece()


def _mixer_call(x, mods, norm_g, w_in, w_out, attn_lam, subln_g, conv_w, chunk_ws, chunk_bias,
                layer, mod_row0, rows_per_batch, seq, nseq, cache=None, prev_kv=None, side=()):
    rows_total = x.shape[0]
    rows = seq * nseq
    nsteps = rows_total // rows
    rope = cache is not None
    lam_init = 0.8 - 0.6 * math.exp(-0.3 * layer)
    resident = dict(pipeline_mode=pl.Buffered(1))
    const2 = lambda i: (0, 0)
    in_specs = [
        pl.BlockSpec((rows, D_MODEL), lambda i: (i, 0)),
        _mod_spec(layer, 1, mod_row0, rows, rows_per_batch),
        pl.BlockSpec((None, 3, D_MODEL), lambda i: (layer, 0, 0)),
        pl.BlockSpec((D_MODEL, IN_WIDTH), const2, **resident),
        pl.BlockSpec((D_MODEL, D_MODEL), const2, **resident),
        pl.BlockSpec((None, 4, ATTN_DH), lambda i: (layer, 0, 0)),
        pl.BlockSpec((None, ATTN_HEADS, HEAD_W), lambda i: (layer, 0, 0)),
        pl.BlockSpec((None, 3, CONV_CH), lambda i: (layer, 0, 0)),
        pl.BlockSpec((None, CHUNK_GROUPS, CHUNK, CHUNK), lambda i: (layer, 0, 0, 0)),
        pl.BlockSpec((None, CHUNK, CHUNK_CH), lambda i: (layer, 0, 0)),
    ]
    args = [x, mods, norm_g, w_in, w_out, attn_lam, subln_g, conv_w, chunk_ws, chunk_bias]
    x_out = jax.ShapeDtypeStruct((rows_total, D_MODEL), F32)
    x_spec = pl.BlockSpec((rows, D_MODEL), lambda i: (i, 0))
    if rope:
        cache_k, cache_v, cos_t, sin_t = cache
        past = cache_k.shape[3]
        kv_spec = pl.BlockSpec((None, None, ATTN_HEADS, past, HEAD_W), lambda i: (i, layer, 0, 0, 0))
        in_specs += [kv_spec, kv_spec,
                     pl.BlockSpec((seq, HEAD_W), const2), pl.BlockSpec((seq, HEAD_W), const2)]
        args += [cache_k, cache_v, cos_t, sin_t]
        out_shape, out_specs = [x_out], [x_spec]
        kv_rows = rows + past
    else:
        kv_rows = rows
        nb = rows_total // seq
        slab = lambda n: pl.BlockSpec((nseq, n, ATTN_HEADS, seq, HEAD_W), lambda i: (i, 0, 0, 0, 0))
        if layer > 0:
            in_specs += [slab(layer), slab(layer)]
            args += list(prev_kv)
        kv_out = jax.ShapeDtypeStruct((nb, layer + 1, ATTN_HEADS, seq, HEAD_W), F32)
        out_shape, out_specs = [x_out, kv_out, kv_out], [x_spec, slab(layer + 1), slab(layer + 1)]
    side_in, side_out, side_shapes = _side_cast_io(side, nsteps)
    args += [w for w, _ in side]
    return pl.pallas_call(
        functools.partial(_mixer_kernel, seq=seq, nseq=nseq, lam_init=lam_init, rope=rope, layer=layer,
                          n_side=len(side)),
        grid=(nsteps,),
        in_specs=in_specs + side_in,
        out_specs=out_specs + side_out,
        out_shape=out_shape + side_shapes,
        scratch_shapes=[pltpu.VMEM((rows, QKV_W), BF16)] * 2 + [pltpu.VMEM((kv_rows, QKV_W), BF16)] * 2
        + [pltpu.VMEM((rows, D_MODEL), BF16)],
        compiler_params=pltpu.CompilerParams(
            dimension_semantics=("arbitrary",), vmem_limit_bytes=V7X_VMEM_LIMIT_BYTES),
        name="mixer_latent" if rope else "mixer_context",
    )(*args)


def _rope_tables(n_tokens):
    n_rows = n_tokens // GRID_W
    row = np.repeat(np.arange(n_rows, dtype=np.float32), GRID_W)
    col = np.tile(np.arange(GRID_W, dtype=np.float32), n_rows)
    inv = np.float32(ROPE_BASE) ** (-np.arange(ROPE_FREQS, dtype=np.float32) / np.float32(ROPE_FREQS))
    ang = np.concatenate([row[:, None] * inv, col[:, None] * inv], axis=-1).astype(np.float32)
    cos, sin = np.cos(ang), np.sin(ang)
    return (jnp.asarray(np.tile(cos, (1, 4)), F32),
            jnp.asarray(np.tile(np.concatenate([-sin, sin], axis=-1), (1, 2)), F32))


def kernel(x_prompt, x_sample, cache_k, cache_v, c, c_ctx, w_mod, b_mod, norm_g, ffn1_w_gu, ffn1_w_d,
           ffn2_w_gu, ffn2_w_d, w_in, w_out, attn_lam, attn_subln_g, conv_w, chunk_ws, chunk_b,
           final_norm_g):
    nb_ctx, seq_ctx, _ = x_prompt.shape
    nb_lat, seq_lat, _ = x_sample.shape

    cond = jnp.concatenate(
        [c_ctx[None, :], c, jnp.zeros((COND_ROWS - 1 - nb_lat, D_MODEL), F32)], axis=0)
    mods = _mod_call(cond, w_mod, b_mod)

    chunk_bias = jnp.repeat(jnp.swapaxes(chunk_b, 1, 2), CHUNK_GDIM, axis=2)
    rope = _rope_tables(seq_lat)
    final_g = final_norm_g[None, :]

    xp = x_prompt.reshape(nb_ctx * seq_ctx, D_MODEL)
    xs = x_sample.reshape(nb_lat * seq_lat, D_MODEL)
    ctx_rows = dict(mod_row0=0, rows_per_batch=nb_ctx * seq_ctx)
    lat_rows = dict(mod_row0=1, rows_per_batch=seq_lat)
    new_kv = None
    chain = [(0, 0, (ffn1_w_gu, 0), (ffn1_w_d, 0))]
    for l in range(DEPTH):
        last = l == DEPTH - 1
        xp, = _ffn_call(xp, mods, norm_g, chain, **ctx_rows)
        next_ffn1 = [] if last else [(ffn1_w_gu, l + 1), (ffn1_w_d, l + 1)]
        xs, w_in_l, w_out_l, *w1_next = _ffn_call(xs, mods, norm_g, chain, **lat_rows,
                                                  side=[(w_in, l), (w_out, l)] + next_ffn1)
        xp, *new_kv, w2_gu, w2_d = _mixer_call(
            xp, mods, norm_g, w_in_l, w_out_l, attn_lam, attn_subln_g, conv_w, chunk_ws, chunk_bias,
            l, seq=seq_ctx, nseq=CTX_SEQ_PER_STEP, prev_kv=new_kv,
            side=[(ffn2_w_gu, l), (ffn2_w_d, l)], **ctx_rows)
        xs, = _mixer_call(xs, mods, norm_g, w_in_l, w_out_l, attn_lam, attn_subln_g, conv_w, chunk_ws,
                          chunk_bias, l, seq=seq_lat, nseq=1, cache=(cache_k, cache_v) + rope, **lat_rows)
        chain = [(l, 2, w2_gu, w2_d)] + ([] if last else [(l + 1, 0, *w1_next)])
    xp, = _ffn_call(xp, mods, norm_g, chain, final_g=final_g, **ctx_rows)
    xs, = _ffn_call(xs, mods, norm_g, chain, final_g=final_g, **lat_rows)

    return (xp.reshape(nb_ctx, seq_ctx, D_MODEL), xs.reshape(nb_lat, seq_lat, D_MODEL), *new_kv)
```
